```python
import jax, jax.numpy as jnp
from jax import lax
import numpy as np

D_MODEL = 2048
BATCH = 8
SEQ = 2048
DEPTH = 2

CHUNK = 64
N_MIXERS = 2
N_LAYERS_A = (DEPTH + 1) // 2
N_LAYERS_B = DEPTH // 2
D_FF = 4 * D_MODEL
EPS = 1e-6

A_HEADS = 16
A_NOPE = 128
A_ROPE = 64
A_V = D_MODEL // A_HEADS
A_Q_RANK = D_MODEL // 4
A_KV_RANK = D_MODEL // 8
IDX_HEADS = 16
IDX_DIM = 64
IDX_ROPE = 32
TOPK_MAX = 256
Q_BLOCK = 128
ROPE_BASE = 10000.0
A_SCALE = (A_NOPE + A_ROPE) ** -0.5
A_SPLITS = [A_Q_RANK, A_Q_RANK + A_KV_RANK, A_Q_RANK + A_KV_RANK + A_ROPE,
            A_Q_RANK + A_KV_RANK + A_ROPE + IDX_DIM]
A_IN = A_Q_RANK + A_KV_RANK + A_ROPE + IDX_DIM + IDX_HEADS

R_HEAD = 64
R_HEADS = D_MODEL // R_HEAD
R_DECAY_LORA = 96
R_AAA_LORA = 96
R_GATE_LORA = 256
R_GN_EPS = R_HEAD * 1e-5

kernel_name = "hybrid_dsa_rwkv7_adaln_trunk"


def rms_norm(x):
    xf = x.astype(jnp.float32)
    return (xf * lax.rsqrt(jnp.mean(xf * xf, axis=-1, keepdims=True) + EPS)).astype(x.dtype)


def layer_norm(x, eps):
    xf = x.astype(jnp.float32)
    mu = jnp.mean(xf, axis=-1, keepdims=True)
    var = jnp.mean(jnp.square(xf - mu), axis=-1, keepdims=True)
    return ((xf - mu) * lax.rsqrt(var + eps)).astype(x.dtype)


def rope_angles(positions, dim):
    inv = 1.0 / (ROPE_BASE ** (jnp.arange(0, dim, 2, dtype=jnp.float32) / dim))
    ang = positions.astype(jnp.float32)[..., None] * inv
    return jnp.cos(ang), jnp.sin(ang)


def apply_rope(x, cos, sin):
    x1, x2 = jnp.split(x, 2, axis=-1)
    c = cos[:, :, None, :].astype(x.dtype)
    s = sin[:, :, None, :].astype(x.dtype)
    return jnp.concatenate([x1 * c - x2 * s, x2 * c + x1 * s], axis=-1)


def dsa_mixer(h, cos_a, sin_a, cos_i, sin_i, w_in, q_norm_g, kv_norm_g, w_uq, w_qidx,
              kidx_ln_g, kidx_ln_b, w_uk, w_uv, w_o):
    B, S, _ = h.shape
    topk = min(TOPK_MAX, S // 4)
    proj = h @ w_in
    c_q, c_kv, k_rope, k_idx, w_idx = jnp.split(proj, A_SPLITS, axis=-1)
    c_q = rms_norm(c_q) * q_norm_g
    c_kv = rms_norm(c_kv) * kv_norm_g
    q = (c_q @ w_uq).reshape(B, S, A_HEADS, A_NOPE + A_ROPE)
    q_nope = q[..., :A_NOPE]
    q_rope = apply_rope(q[..., A_NOPE:], cos_a, sin_a)
    k_rope = apply_rope(k_rope[:, :, None, :], cos_a, sin_a)[:, :, 0, :]
    keys = jnp.concatenate([c_kv, k_rope], axis=-1)
    q_idx = (c_q @ w_qidx).reshape(B, S, IDX_HEADS, IDX_DIM)
    q_idx = jnp.concatenate([apply_rope(q_idx[..., :IDX_ROPE], cos_i, sin_i), q_idx[..., IDX_ROPE:]], axis=-1)
    k_idx = layer_norm(k_idx, EPS) * kidx_ln_g + kidx_ln_b
    k_idx = jnp.concatenate([apply_rope(k_idx[:, :, None, :IDX_ROPE], cos_i, sin_i)[:, :, 0, :],
                             k_idx[..., IDX_ROPE:]], axis=-1)
    w_idx = w_idx * (IDX_HEADS ** -0.5 * IDX_DIM ** -0.5)

    nb = S // Q_BLOCK

    def to_blocks(t):
        return jnp.moveaxis(t.reshape((B, nb, Q_BLOCK) + t.shape[2:]), 1, 0)

    def block(args):
        bi, qn, qr, qi, wi = args
        t = bi * Q_BLOCK + jnp.arange(Q_BLOCK)
        q_chunk = t // CHUNK
        allowed = (jnp.arange(S)[None, :] // CHUNK) <= q_chunk[:, None]
        rel = jax.nn.relu(jnp.einsum('bqhd,bsd->bqhs', qi, k_idx))
        score = jnp.einsum('bqhs,bqh->bqs', rel, wi).astype(jnp.float32)
        score = jnp.where(allowed[None], score, -jnp.inf)
        _, idx = lax.top_k(score, topk)
        valid = (idx // CHUNK) <= q_chunk[None, :, None]
        sel = jax.vmap(lambda kb, ib: kb[ib])(keys, idx)
        c_sel = sel[..., :A_KV_RANK]
        kr_sel = sel[..., A_KV_RANK:]
        q_lat = jnp.einsum('bqhd,rhd->bqhr', qn, w_uk)
        logits = (jnp.einsum('bqhr,bqkr->bqhk', q_lat, c_sel)
                  + jnp.einsum('bqhd,bqkd->bqhk', qr, kr_sel)).astype(jnp.float32) * A_SCALE
        logits = jnp.where(valid[:, :, None, :], logits, -jnp.inf)
        p = jax.nn.softmax(logits, axis=-1).astype(h.dtype)
        o_lat = jnp.einsum('bqhk,bqkr->bqhr', p, c_sel)
        o = jnp.einsum('bqhr,rhd->bqhd', o_lat, w_uv)
        return o.reshape(B, Q_BLOCK, A_HEADS * A_V)

    out = lax.map(block, (jnp.arange(nb), to_blocks(q_nope), to_blocks(q_rope),
                          to_blocks(q_idx), to_blocks(w_idx)))
    out = jnp.moveaxis(out, 0, 1).reshape(B, S, A_HEADS * A_V)
    return out @ w_o


def rwkv7_mixer(h, mu, w_r, w_k, w_v, w_o, w0, w_w1, w_w2, a0, w_a1, w_a2, w_g1, w_g2,
                k_k, k_a, r_k, gn_g, gn_b):
    B, S, D = h.shape
    h_prev = jnp.pad(h, ((0, 0), (1, 0), (0, 0)))[:, :-1]
    delta = h_prev - h
    xr = h + delta * mu[0]
    xw = h + delta * mu[1]
    xk = h + delta * mu[2]
    xv = h + delta * mu[3]
    xa = h + delta * mu[4]
    xg = h + delta * mu[5]
    r = xr @ w_r
    k = xk @ w_k
    v = xv @ w_v
    w_log = -jax.nn.softplus(-(w0 + jnp.tanh(xw @ w_w1) @ w_w2)) - 0.5
    decay = jnp.exp(-jnp.exp(w_log.astype(jnp.float32)))
    a = jax.nn.sigmoid(a0 + (xa @ w_a1) @ w_a2)
    g = jax.nn.sigmoid(xg @ w_g1) @ w_g2

    def heads(t):
        return t.reshape(B, S, R_HEADS, R_HEAD)

    kk = heads(k * k_k).astype(jnp.float32)
    kk = kk / jnp.maximum(jnp.sqrt(jnp.sum(kk * kk, axis=-1, keepdims=True)), 1e-12)
    k = k * (1 + (a - 1) * k_a)
    rh, kh, vh, ah, dh = heads(r), heads(k), heads(v), heads(a), heads(decay)

    def step(state, inp):
        r_t, k_t, v_t, w_t, kk_t, a_t = inp
        sa = jnp.einsum('bhij,bhj->bhi', state, -kk_t)
        state = (state * w_t[:, :, None, :] + sa[..., None] * (kk_t * a_t)[:, :, None, :]
                 + v_t[..., None] * k_t[:, :, None, :])
        return state, jnp.einsum('bhij,bhj->bhi', state, r_t)

    def seq_first(t):
        return jnp.moveaxis(t.astype(jnp.float32), 1, 0)

    state0 = jnp.zeros((B, R_HEADS, R_HEAD, R_HEAD), jnp.float32)
    _, o = lax.scan(step, state0, (seq_first(rh), seq_first(kh), seq_first(vh),
                                   seq_first(dh), seq_first(kk), seq_first(ah)))
    o = jnp.moveaxis(o, 0, 1)
    o = layer_norm(o, R_GN_EPS).reshape(B, S, D).astype(h.dtype) * gn_g + gn_b
    bonus = jnp.sum(rh * kh * r_k, axis=-1, keepdims=True) * vh
    o = o + bonus.reshape(B, S, D)
    return (o * g) @ w_o


def setup_inputs(seed: int = 0) -> dict:
    key = jax.random.key(seed)
    ks = iter(jax.random.split(key, 48))

    def nrm(shape, scale):
        return jax.random.normal(next(ks), shape, jnp.float32) * scale

    def gain(shape):
        return 1.0 + nrm(shape, 0.02)

    NA, NB, D = N_LAYERS_A, N_LAYERS_B, D_MODEL
    start = jax.random.randint(next(ks), (BATCH, 1), 0, 1024, dtype=jnp.int32)
    positions = start + jnp.arange(SEQ, dtype=jnp.int32)[None, :]
    return {
        "x": nrm((BATCH, SEQ, D), 1.0),
        "c": nrm((BATCH, D), 1.0),
        "positions": positions,
        "ada_w": nrm((DEPTH, D, 6 * D), 0.5 * D ** -0.5),
        "ada_b": nrm((DEPTH, 6 * D), 0.02),
        "mlp_w1": nrm((DEPTH, D, D_FF), D ** -0.5),
        "mlp_w2": nrm((DEPTH, D_FF, D), D_FF ** -0.5),
        "final_g": gain((D,)),
        "a_w_in": nrm((NA, D, A_IN), D ** -0.5),
        "a_q_norm_g": gain((NA, A_Q_RANK)),
        "a_kv_norm_g": gain((NA, A_KV_RANK)),
        "a_w_uq": nrm((NA, A_Q_RANK, A_HEADS * (A_NOPE + A_ROPE)), A_Q_RANK ** -0.5),
        "a_w_qidx": nrm((NA, A_Q_RANK, IDX_HEADS * IDX_DIM), A_Q_RANK ** -0.5),
        "a_kidx_ln_g": gain((NA, IDX_DIM)),
        "a_kidx_ln_b": nrm((NA, IDX_DIM), 0.02),
        "a_w_uk": nrm((NA, A_KV_RANK, A_HEADS, A_NOPE), A_KV_RANK ** -0.5),
        "a_w_uv": nrm((NA, A_KV_RANK, A_HEADS, A_V), A_KV_RANK ** -0.5),
        "a_w_o": nrm((NA, A_HEADS * A_V, D), (A_HEADS * A_V) ** -0.5),
        "b_mu": jax.random.uniform(next(ks), (NB, 6, D), jnp.float32),
        "b_w_r": nrm((NB, D, D), D ** -0.5),
        "b_w_k": nrm((NB, D, D), D ** -0.5),
        "b_w_v": nrm((NB, D, D), D ** -0.5),
        "b_w_o": nrm((NB, D, D), D ** -0.5),
        "b_w0": nrm((NB, D), 1.0) - 0.5,
        "b_w_w1": nrm((NB, D, R_DECAY_LORA), D ** -0.5),
        "b_w_w2": nrm((NB, R_DECAY_LORA, D), 0.5 * R_DECAY_LORA ** -0.5),
        "b_a0": nrm((NB, D), 0.5),
        "b_w_a1": nrm((NB, D, R_AAA_LORA), D ** -0.5),
        "b_w_a2": nrm((NB, R_AAA_LORA, D), 0.5 * R_AAA_LORA ** -0.5),
        "b_w_g1": nrm((NB, D, R_GATE_LORA), D ** -0.5),
        "b_w_g2": nrm((NB, R_GATE_LORA, D), R_GATE_LORA ** -0.5),
        "b_k_k": 0.85 + nrm((NB, D), 0.05),
        "b_k_a": 1.0 + nrm((NB, D), 0.05),
        "b_r_k": nrm((NB, R_HEADS, R_HEAD), 0.1),
        "b_gn_g": gain((NB, D)),
        "b_gn_b": nrm((NB, D), 0.02),
    }


def reference(x, c, positions, ada_w, ada_b, mlp_w1, mlp_w2, final_g,
              a_w_in, a_q_norm_g, a_kv_norm_g, a_w_uq, a_w_qidx, a_kidx_ln_g, a_kidx_ln_b,
              a_w_uk, a_w_uv, a_w_o,
              b_mu, b_w_r, b_w_k, b_w_v, b_w_o, b_w0, b_w_w1, b_w_w2, b_a0, b_w_a1, b_w_a2,
              b_w_g1, b_w_g2, b_k_k, b_k_a, b_r_k, b_gn_g, b_gn_b):
    cos_a, sin_a = rope_angles(positions, A_ROPE)
    cos_i, sin_i = rope_angles(positions, IDX_ROPE)
    c_act = jax.nn.silu(c)
    for i in range(DEPTH):
        mod = c_act @ ada_w[i] + ada_b[i]
        sh1, sc1, g1, sh2, sc2, g2 = [m[:, None, :] for m in jnp.split(mod, 6, axis=-1)]
        hmix = rms_norm(x) * (1 + sc1) + sh1
        j = i // N_MIXERS
        if i % N_MIXERS == 0:
            y = dsa_mixer(hmix, cos_a, sin_a, cos_i, sin_i, a_w_in[j], a_q_norm_g[j], a_kv_norm_g[j],
                          a_w_uq[j], a_w_qidx[j], a_kidx_ln_g[j], a_kidx_ln_b[j],
                          a_w_uk[j], a_w_uv[j], a_w_o[j])
        else:
            y = rwkv7_mixer(hmix, b_mu[j], b_w_r[j], b_w_k[j], b_w_v[j], b_w_o[j], b_w0[j],
                            b_w_w1[j], b_w_w2[j], b_a0[j], b_w_a1[j], b_w_a2[j], b_w_g1[j], b_w_g2[j],
                            b_k_k[j], b_k_a[j], b_r_k[j], b_gn_g[j], b_gn_b[j])
        x = x + g1 * y
        hff = rms_norm(x) * (1 + sc2) + sh2
        x = x + g2 * (jnp.square(jax.nn.relu(hff @ mlp_w1[i])) @ mlp_w2[i])
    return rms_norm(x) * final_g
```

```python
import functools

import jax
import jax.numpy as jnp
from jax import lax
from jax.experimental import pallas as pl
from jax.experimental.pallas import tpu as pltpu

F32 = jnp.float32
BF16 = jnp.bfloat16

D_MODEL = 2048
DEPTH = 2
CHUNK = 64
EPS = 1e-6
A_HEADS = 16
A_NOPE = 128
A_ROPE = 64
A_V = 128
A_Q_RANK = 512
A_KV_RANK = 256
A_KEY = A_KV_RANK + A_ROPE
IDX_HEADS = 16
IDX_DIM = 64
IDX_ROPE = 32
TOPK_MAX = 256
Q_BLOCK = 128
ROPE_BASE = 10000.0
A_SCALE = (A_NOPE + A_ROPE) ** -0.5
A_IN = A_Q_RANK + A_KV_RANK + A_ROPE + IDX_DIM + IDX_HEADS
A_IN_PAD = 1024
R_HEAD = 64
R_GN_EPS = R_HEAD * 1e-5
LORA_PAD = 128
R_GATE_LORA = 256

VMEM_LIMIT_BYTES = 56 * 1024 * 1024
MASK_VALUE = -1e30

NT_DIMS = (((1,), (1,)), ((), ()))
TN_DIMS = (((0,), (0,)), ((), ()))
NN_DIMS = (((1,), (0,)), ((), ()))


def _params(*sem):
    return pltpu.CompilerParams(dimension_semantics=sem, vmem_limit_bytes=VMEM_LIMIT_BYTES)


def _dot(a, b, dims=NN_DIMS):
    return lax.dot_general(a, b, dims, preferred_element_type=F32)


def _split(x):
    hi = x.astype(BF16)
    lo = (x - hi.astype(F32)).astype(BF16)
    return hi, lo


def _dot3(a, b, dims=NN_DIMS):
    ah, al = _split(a)
    bh, bl = _split(b)
    return _dot(ah, bh, dims) + _dot(ah, bl, dims) + _dot(al, bh, dims)


def _dot2_exact_rhs(a, b_bf16, dims=NN_DIMS):
    ah, al = _split(a)
    return _dot(ah, b_bf16, dims) + _dot(al, b_bf16, dims)


def _rms(x):
    return x * lax.rsqrt(jnp.mean(x * x, axis=-1, keepdims=True) + EPS)


def _rms_mod(x, sc, sh):
    return _rms(x) * (1.0 + sc) + sh


def _sigmoid(x):
    return 1.0 / (1.0 + jnp.exp(-x))


def _rope_half(x, c, s):
    d = x.shape[-1] // 2
    x1 = x[:, :d]
    x2 = x[:, d:]
    return jnp.concatenate([x1 * c - x2 * s, x2 * c + x1 * s], axis=-1)


def _mod_kernel(c_ref, w_ref, b_ref, o_ref):
    c = c_ref[...]
    a = (c * _sigmoid(c)).astype(BF16)
    o_ref[0] = _dot(a, w_ref[0].astype(BF16)) + b_ref[0]


def _ada_mod(c, ada_w, ada_b):
    depth, d, n = ada_w.shape
    b = c.shape[0]
    tn = 1024
    return pl.pallas_call(
        _mod_kernel,
        grid=(depth, n // tn),
        in_specs=[
            pl.BlockSpec((b, d), lambda l, j: (0, 0)),
            pl.BlockSpec((1, d, tn), lambda l, j: (l, 0, j)),
            pl.BlockSpec((1, 1, tn), lambda l, j: (l, 0, j)),
        ],
        out_specs=pl.BlockSpec((1, b, tn), lambda l, j: (l, 0, j)),
        out_shape=jax.ShapeDtypeStruct((depth, b, n), F32),
        compiler_params=_params("arbitrary", "arbitrary"),
    )(c, ada_w, ada_b.reshape(depth, 1, n))


def _mlp_kernel(x_ref, sc_ref, sh_ref, g_ref, w1_ref, w2_ref, fg_ref, o_ref, h_scr, acc_scr, *, final_norm):
    f = pl.program_id(2)

    @pl.when(f == 0)
    def _():
        h_scr[...] = _rms_mod(x_ref[0], sc_ref[0], sh_ref[0]).astype(BF16)
        acc_scr[...] = jnp.zeros_like(acc_scr)

    a = jnp.maximum(_dot(h_scr[...], w1_ref[...]), 0.0)
    acc_scr[...] += _dot((a * a).astype(BF16), w2_ref[...])

    @pl.when(f == pl.num_programs(2) - 1)
    def _():
        y = x_ref[0] + g_ref[0] * acc_scr[...]
        if final_norm:
            y = _rms(y) * fg_ref[...]
        o_ref[0] = y


def _mlp(x, sc, sh, g, w1, w2, final_g, final_norm, tm=512, tf=1024):
    b, s, d = x.shape
    dff = w1.shape[1]
    kern = functools.partial(_mlp_kernel, final_norm=final_norm)
    vec = pl.BlockSpec((1, 1, d), lambda bi, i, f: (bi, 0, 0))
    return pl.pallas_call(
        kern,
        grid=(b, s // tm, dff // tf),
        in_specs=[
            pl.BlockSpec((1, tm, d), lambda bi, i, f: (bi, i, 0)),
            vec, vec, vec,
            pl.BlockSpec((d, tf), lambda bi, i, f: (0, f)),
            pl.BlockSpec((tf, d), lambda bi, i, f: (f, 0)),
            pl.BlockSpec((1, d), lambda bi, i, f: (0, 0)),
        ],
        out_specs=pl.BlockSpec((1, tm, d), lambda bi, i, f: (bi, i, 0)),
        out_shape=jax.ShapeDtypeStruct((b, s, d), F32),
        scratch_shapes=[pltpu.VMEM((tm, d), BF16), pltpu.VMEM((tm, d), F32)],
        compiler_params=_params("parallel", "parallel", "arbitrary"),
    )(x, sc, sh, g, w1, w2, final_g.reshape(1, d))


def _dsa_in_kernel(x_ref, sc_ref, sh_ref, w_ref, qg_ref, kvg_ref, lng_ref, lnb_ref,
                   ca_ref, sa_ref, ci_ref, si_ref, cq_ref, keys_ref, kidx_ref, widx_ref):
    h = _rms_mod(x_ref[0], sc_ref[0], sh_ref[0]).astype(BF16)
    proj = _dot(h, w_ref[...])
    o_kv = A_Q_RANK
    o_kr = o_kv + A_KV_RANK
    o_ki = o_kr + A_ROPE
    o_wi = o_ki + IDX_DIM
    cq_ref[0] = (_rms(proj[:, :o_kv]) * qg_ref[...]).astype(BF16)
    keys_ref[0, :, :A_KV_RANK] = (_rms(proj[:, o_kv:o_kr]) * kvg_ref[...]).astype(BF16)
    keys_ref[0, :, A_KV_RANK:] = _rope_half(proj[:, o_kr:o_ki], ca_ref[0], sa_ref[0]).astype(BF16)
    ki = proj[:, o_ki:o_wi]
    mu = jnp.mean(ki, axis=-1, keepdims=True)
    kc = ki - mu
    var = jnp.mean(kc * kc, axis=-1, keepdims=True)
    ki = kc * lax.rsqrt(var + EPS) * lng_ref[...] + lnb_ref[...]
    ki = jnp.concatenate([_rope_half(ki[:, :IDX_ROPE], ci_ref[0], si_ref[0]), ki[:, IDX_ROPE:]], axis=-1)
    kidx_ref[0] = ki.astype(BF16)
    widx_ref[0] = proj[:, o_wi:o_wi + IDX_HEADS] * (IDX_HEADS ** -0.5 * IDX_DIM ** -0.5)


def _dsa_in(x, sc, sh, w_in_p, qg, kvg, lng, lnb, cos_a, sin_a, cos_i, sin_i, tm=512):
    b, s, d = x.shape
    vec = pl.BlockSpec((1, 1, d), lambda bi, i: (bi, 0, 0))

    def full(a):
        return pl.BlockSpec(a.shape, lambda bi, i: (0,) * a.ndim)

    def tok(w):
        return pl.BlockSpec((1, tm, w), lambda bi, i: (bi, i, 0))

    return pl.pallas_call(
        _dsa_in_kernel,
        grid=(b, s // tm),
        in_specs=[tok(d), vec, vec, full(w_in_p), full(qg), full(kvg), full(lng), full(lnb),
                  tok(A_ROPE // 2), tok(A_ROPE // 2), tok(IDX_ROPE // 2), tok(IDX_ROPE // 2)],
        out_specs=[tok(A_Q_RANK), tok(A_KEY), tok(IDX_DIM), tok(IDX_HEADS)],
        out_shape=[jax.ShapeDtypeStruct((b, s, A_Q_RANK), BF16),
                   jax.ShapeDtypeStruct((b, s, A_KEY), BF16),
                   jax.ShapeDtypeStruct((b, s, IDX_DIM), BF16),
                   jax.ShapeDtypeStruct((b, s, IDX_HEADS), F32)],
        compiler_params=_params("parallel", "parallel"),
    )(x, sc, sh, w_in_p, qg, kvg, lng, lnb, cos_a, sin_a, cos_i, sin_i)


def _fold_kernel(a_ref, b_ref, o_ref):
    o_ref[0] = _dot3(a_ref[0], b_ref[0]).astype(BF16)


def _fold_uq_uk(w_uq_nope, w_uk_t):
    h, rq, dn = w_uq_nope.shape
    rkv = w_uk_t.shape[2]
    return pl.pallas_call(
        _fold_kernel,
        grid=(h,),
        in_specs=[pl.BlockSpec((1, rq, dn), lambda i: (i, 0, 0)),
                  pl.BlockSpec((1, dn, rkv), lambda i: (i, 0, 0))],
        out_specs=pl.BlockSpec((1, rq, rkv), lambda i: (i, 0, 0)),
        out_shape=jax.ShapeDtypeStruct((h, rq, rkv), BF16),
        compiler_params=_params("arbitrary"),
    )(w_uq_nope, w_uk_t)


def _dsa_q_kernel(cq_ref, w_ref, cq_t, sq_t, ci_t, si_t, qatt_ref, qidx_ref):
    cq = cq_ref[0]
    tm = cq.shape[0]
    nb = tm // Q_BLOCK
    n_lat = A_HEADS * A_KV_RANK
    n_rope = A_HEADS * A_ROPE
    n_idx = IDX_HEADS * IDX_DIM
    for h in range(A_HEADS):
        lat = _dot(cq, w_ref[:, h * A_KV_RANK:(h + 1) * A_KV_RANK]) * A_SCALE
        qatt_ref[0, :, h, :, :A_KV_RANK] = lat.astype(BF16).reshape(nb, Q_BLOCK, A_KV_RANK)
    o = n_lat
    r = _dot(cq, w_ref[:, o:o + n_rope])
    rs = _dot(cq, w_ref[:, o + n_rope:o + 2 * n_rope])
    rep = n_rope // cq_t.shape[2]
    qr = ((r * jnp.tile(cq_t[0], (1, rep)) + rs * jnp.tile(sq_t[0], (1, rep))) * A_SCALE).astype(BF16)
    for h in range(A_HEADS):
        qatt_ref[0, :, h, :, A_KV_RANK:] = qr[:, h * A_ROPE:(h + 1) * A_ROPE].reshape(nb, Q_BLOCK, A_ROPE)
    o = n_lat + 2 * n_rope
    qi = _dot(cq, w_ref[:, o:o + n_idx])
    qis = _dot(cq, w_ref[:, o + n_idx:o + 2 * n_idx])
    rep = n_idx // ci_t.shape[2]
    qidx_ref[0] = (qi * jnp.tile(ci_t[0], (1, rep)) + qis * jnp.tile(si_t[0], (1, rep))).astype(BF16)


def _dsa_q(cq, w_q, cq_t, sq_t, ci_t, si_t, tm=512):
    b, s, rq = cq.shape
    nb = tm // Q_BLOCK

    def tok(w):
        return pl.BlockSpec((1, tm, w), lambda bi, i: (bi, i, 0))

    return pl.pallas_call(
        _dsa_q_kernel,
        grid=(b, s // tm),
        in_specs=[tok(rq), pl.BlockSpec(w_q.shape, lambda bi, i: (0, 0)),
                  tok(128), tok(128), tok(128), tok(128)],
        out_specs=[pl.BlockSpec((1, nb, A_HEADS, Q_BLOCK, A_KEY), lambda bi, i: (bi, i, 0, 0, 0)),
                   tok(IDX_HEADS * IDX_DIM)],
        out_shape=[jax.ShapeDtypeStruct((b, s // Q_BLOCK, A_HEADS, Q_BLOCK, A_KEY), BF16),
                   jax.ShapeDtypeStruct((b, s, IDX_HEADS * IDX_DIM), BF16)],
        compiler_params=_params("parallel", "parallel"),
    )(cq, w_q, cq_t, sq_t, ci_t, si_t)


def _dsa_idx_kernel(qidx_ref, kidx_ref, widx_ref, bias_ref, key_scr, *, topk):
    qb = pl.program_id(1)
    s = kidx_ref.shape[1]
    kidx = kidx_ref[0]
    w = widx_ref[0]
    score = jnp.zeros((Q_BLOCK, s), F32)
    for h in range(IDX_HEADS):
        rel = _dot(qidx_ref[0, :, h * IDX_DIM:(h + 1) * IDX_DIM], kidx, NT_DIMS)
        score = score + jnp.maximum(rel, 0.0) * w[:, h:h + 1]
    row = qb * Q_BLOCK + lax.broadcasted_iota(jnp.int32, (Q_BLOCK, s), 0)
    col = lax.broadcasted_iota(jnp.int32, (Q_BLOCK, s), 1)
    allowed = (col // CHUNK) <= (row // CHUNK)
    score = jnp.where(score == 0.0, 0.0, score)
    score = jnp.where(allowed, score, -jnp.inf)
    bits = pltpu.bitcast(score, jnp.int32)
    key_scr[...] = bits ^ ((bits >> 31) & jnp.int32(0x7FFFFFFF))

    def count_ge(cand):
        return jnp.sum(jnp.where(key_scr[...] >= cand, 1.0, 0.0), axis=1, keepdims=True)

    kf = jnp.float32(topk)
    int_min = jnp.int32(-2 ** 31)
    thr = jnp.where(count_ge(jnp.zeros((Q_BLOCK, 1), jnp.int32)) >= kf, jnp.int32(0), int_min)

    def body(i, thr):
        cand = thr | lax.shift_left(jnp.int32(1), jnp.int32(30) - i)
        return jnp.where(count_ge(cand) >= kf, cand, thr)

    thr = lax.fori_loop(0, 31, body, thr)
    sel = (key_scr[...] >= thr) & allowed
    bias_ref[0] = jnp.where(sel, 0.0, MASK_VALUE).astype(BF16)


def _dsa_idx(qidx, kidx, widx, topk):
    b, s, _ = qidx.shape
    kern = functools.partial(_dsa_idx_kernel, topk=topk)
    return pl.pallas_call(
        kern,
        grid=(b, s // Q_BLOCK),
        in_specs=[pl.BlockSpec((1, Q_BLOCK, IDX_HEADS * IDX_DIM), lambda bi, i: (bi, i, 0)),
                  pl.BlockSpec((1, s, IDX_DIM), lambda bi, i: (bi, 0, 0)),
                  pl.BlockSpec((1, Q_BLOCK, IDX_HEADS), lambda bi, i: (bi, i, 0))],
        out_specs=pl.BlockSpec((1, Q_BLOCK, s), lambda bi, i: (bi, i, 0)),
        out_shape=jax.ShapeDtypeStruct((b, s, s), BF16),
        scratch_shapes=[pltpu.VMEM((Q_BLOCK, s), jnp.int32)],
        compiler_params=_params("parallel", "parallel"),
    )(qidx, kidx, widx)


def _dsa_attn_kernel(q_ref, keys_ref, bias_ref, o_ref, m_scr, l_scr, acc_scr, *, tk):
    qb = pl.program_id(1)
    rows = A_HEADS * Q_BLOCK
    q = q_ref[0, 0].reshape(rows, A_KEY)
    m_scr[...] = jnp.full_like(m_scr, -jnp.inf)
    l_scr[...] = jnp.zeros_like(l_scr)
    acc_scr[...] = jnp.zeros_like(acc_scr)
    n_tiles = ((qb + 1) * Q_BLOCK + tk - 1) // tk

    def body(t, carry):
        off = pl.multiple_of(t * tk, tk)
        kt = keys_ref[0, pl.ds(off, tk), :]
        sc = _dot(q, kt, NT_DIMS)
        bias = bias_ref[0, :, pl.ds(off, tk)].astype(F32)
        sc = (sc.reshape(A_HEADS, Q_BLOCK, tk) + bias[None]).reshape(rows, tk)
        m_prev = m_scr[...]
        m_new = jnp.maximum(m_prev, jnp.max(sc, axis=1, keepdims=True))
        alpha = jnp.exp(m_prev - m_new)
        p = jnp.exp(sc - jnp.tile(m_new, (1, tk // 128)))
        l_scr[...] = alpha * l_scr[...] + jnp.sum(p, axis=1, keepdims=True)
        acc_scr[...] = (acc_scr[...] * jnp.tile(alpha, (1, A_KV_RANK // 128))
                        + _dot(p.astype(BF16), kt[:, :A_KV_RANK]))
        m_scr[...] = m_new
        return carry

    lax.fori_loop(0, n_tiles, body, 0)
    o = acc_scr[...] / jnp.tile(l_scr[...], (1, A_KV_RANK // 128))
    o_ref[0, 0] = o.astype(BF16).reshape(A_HEADS, Q_BLOCK, A_KV_RANK)


def _dsa_attn(qatt, keys, bias, tk=512):
    b, nb = qatt.shape[:2]
    s = keys.shape[1]
    tk = min(tk, s)
    rows = A_HEADS * Q_BLOCK
    kern = functools.partial(_dsa_attn_kernel, tk=tk)
    return pl.pallas_call(
        kern,
        grid=(b, nb),
        in_specs=[pl.BlockSpec((1, 1, A_HEADS, Q_BLOCK, A_KEY), lambda bi, i: (bi, i, 0, 0, 0)),
                  pl.BlockSpec((1, s, A_KEY), lambda bi, i: (bi, 0, 0)),
                  pl.BlockSpec((1, Q_BLOCK, s), lambda bi, i: (bi, i, 0))],
        out_specs=pl.BlockSpec((1, 1, A_HEADS, Q_BLOCK, A_KV_RANK), lambda bi, i: (bi, i, 0, 0, 0)),
        out_shape=jax.ShapeDtypeStruct((b, nb, A_HEADS, Q_BLOCK, A_KV_RANK), BF16),
        scratch_shapes=[pltpu.VMEM((rows, 128), F32), pltpu.VMEM((rows, 128), F32),
                        pltpu.VMEM((rows, A_KV_RANK), F32)],
        compiler_params=_params("parallel", "parallel"),
    )(qatt, keys, bias)


def _dsa_out_kernel(ol_ref, wuv_ref, wo_ref, x_ref, g_ref, o_ref, y_scr):
    nb = ol_ref.shape[1]
    tm = nb * Q_BLOCK
    for h in range(A_HEADS):
        oh = ol_ref[0, :, h].reshape(tm, A_KV_RANK)
        y_scr[:, h * A_V:(h + 1) * A_V] = _dot(oh, wuv_ref[h]).astype(BF16)
    o_ref[0] = x_ref[0] + g_ref[0] * _dot(y_scr[...], wo_ref[...])


def _dsa_out(olat, w_uv_h, w_o, x, g, tm=512):
    b, s, d = x.shape
    nb = tm // Q_BLOCK
    return pl.pallas_call(
        _dsa_out_kernel,
        grid=(b, s // tm),
        in_specs=[pl.BlockSpec((1, nb, A_HEADS, Q_BLOCK, A_KV_RANK), lambda bi, i: (bi, i, 0, 0, 0)),
                  pl.BlockSpec(w_uv_h.shape, lambda bi, i: (0, 0, 0)),
                  pl.BlockSpec(w_o.shape, lambda bi, i: (0, 0)),
                  pl.BlockSpec((1, tm, d), lambda bi, i: (bi, i, 0)),
                  pl.BlockSpec((1, 1, d), lambda bi, i: (bi, 0, 0))],
        out_specs=pl.BlockSpec((1, tm, d), lambda bi, i: (bi, i, 0)),
        out_shape=jax.ShapeDtypeStruct((b, s, d), F32),
        scratch_shapes=[pltpu.VMEM((tm, A_HEADS * A_V), BF16)],
        compiler_params=_params("parallel", "parallel"),
    )(olat, w_uv_h, w_o, x, g)


def _rope_tables(positions):
    def angles(dim):
        inv = 1.0 / (ROPE_BASE ** (jnp.arange(0, dim, 2, dtype=F32) / dim))
        ang = positions.astype(F32)[..., None] * inv
        return jnp.cos(ang), jnp.sin(ang)

    cos_a, sin_a = angles(A_ROPE)
    cos_i, sin_i = angles(IDX_ROPE)
    return cos_a, sin_a, cos_i, sin_i


def _dsa_mixer(x, sc, sh, g, positions, w_in, qg, kvg, w_uq, w_qidx, lng, lnb, w_uk, w_uv, w_o):
    b, s, d = x.shape
    topk = min(TOPK_MAX, s // 4)
    cos_a, sin_a, cos_i, sin_i = _rope_tables(positions)
    w_in_p = jnp.pad(w_in, ((0, 0), (0, A_IN_PAD - A_IN))).astype(BF16)
    cq, keys, kidx, widx = _dsa_in(x, sc, sh, w_in_p, qg.reshape(1, -1), kvg.reshape(1, -1),
                                   lng.reshape(1, -1), lnb.reshape(1, -1), cos_a, sin_a, cos_i, sin_i)

    rq = w_uq.shape[0]
    w_uq_h = w_uq.reshape(rq, A_HEADS, A_NOPE + A_ROPE)
    w_lat = _fold_uq_uk(jnp.transpose(w_uq_h[:, :, :A_NOPE], (1, 0, 2)), jnp.transpose(w_uk, (1, 2, 0)))
    w_lat = jnp.transpose(w_lat, (1, 0, 2)).reshape(rq, A_HEADS * A_KV_RANK)
    w_r = w_uq_h[:, :, A_NOPE:]
    w_rs = jnp.concatenate([w_r[..., A_ROPE // 2:], w_r[..., :A_ROPE // 2]], axis=-1)
    w_i = w_qidx.reshape(rq, IDX_HEADS, IDX_DIM)
    hr = IDX_ROPE // 2
    w_is = jnp.concatenate([w_i[..., hr:IDX_ROPE], w_i[..., :hr], jnp.zeros_like(w_i[..., IDX_ROPE:])], axis=-1)
    w_q = jnp.concatenate([w_lat, w_r.reshape(rq, -1).astype(BF16), w_rs.reshape(rq, -1).astype(BF16),
                           w_i.reshape(rq, -1).astype(BF16), w_is.reshape(rq, -1).astype(BF16)], axis=1)
    cq_t = jnp.concatenate([cos_a, cos_a, cos_a, cos_a], axis=-1)
    sq_t = jnp.concatenate([-sin_a, sin_a, -sin_a, sin_a], axis=-1)
    one = jnp.ones(cos_i.shape[:-1] + (IDX_DIM - IDX_ROPE,), F32)
    ci_t = jnp.concatenate([cos_i, cos_i, one, cos_i, cos_i, one], axis=-1)
    si_t = jnp.concatenate([-sin_i, sin_i, 0 * one, -sin_i, sin_i, 0 * one], axis=-1)
    qatt, qidx = _dsa_q(cq, w_q, cq_t, sq_t, ci_t, si_t)

    bias = _dsa_idx(qidx, kidx, widx, topk)
    olat = _dsa_attn(qatt, keys, bias)
    w_uv_h = jnp.transpose(w_uv, (1, 0, 2)).astype(BF16)
    return _dsa_out(olat, w_uv_h, w_o.astype(BF16), x, g)


def _rwkv_in_kernel(x_ref, xp_ref, sc_ref, sh_ref, mu_ref, wl1_ref, wr_ref, wk_ref, wv_ref,
                    w2_ref, a2_ref, g2_ref, vec_ref, seg_ref,
                    r_out, lw_out, k_out, v_out, kk_out, a_out, g_out, bonus_out,
                    mix_scr, lora_scr):
    i = pl.program_id(1)
    j = pl.program_id(2)

    @pl.when(j == 0)
    def _():
        sc = sc_ref[0]
        sh = sh_ref[0]
        h = _rms_mod(x_ref[0], sc, sh)
        hp = _rms_mod(xp_ref[0, 7:8, :], sc, sh)
        hp = jnp.where(i == 0, 0.0, hp)
        rows = lax.broadcasted_iota(jnp.int32, h.shape, 0)
        h_prev = jnp.where(rows == 0, hp, pltpu.roll(h, 1, axis=0))
        delta = h_prev - h
        for m in range(6):
            mix_scr[m] = (h + delta * mu_ref[m:m + 1, :]).astype(BF16)
        tw = jnp.tanh(_dot(mix_scr[1], wl1_ref[:, :LORA_PAD]))
        ta = _dot(mix_scr[4], wl1_ref[:, LORA_PAD:2 * LORA_PAD])
        tg = _sigmoid(_dot(mix_scr[5], wl1_ref[:, 2 * LORA_PAD:]))
        lora_scr[:, :LORA_PAD] = tw.astype(BF16)
        lora_scr[:, LORA_PAD:2 * LORA_PAD] = ta.astype(BF16)
        lora_scr[:, 2 * LORA_PAD:] = tg.astype(BF16)

    w0 = vec_ref[0:1, :]
    a0 = vec_ref[1:2, :]
    k_k = vec_ref[2:3, :]
    k_a = vec_ref[3:4, :]
    r_k = vec_ref[4:5, :]
    seg = seg_ref[...]
    r = _dot(mix_scr[0], wr_ref[...])
    k = _dot(mix_scr[2], wk_ref[...])
    v = _dot(mix_scr[3], wv_ref[...])
    z = -(w0 + _dot(lora_scr[:, :LORA_PAD], w2_ref[...]))
    softplus = jnp.maximum(z, 0.0) + jnp.log(1.0 + jnp.exp(-jnp.abs(z)))
    lw = -jnp.exp(-softplus - 0.5)
    a = _sigmoid(a0 + _dot(lora_scr[:, LORA_PAD:2 * LORA_PAD], a2_ref[...]))
    g = _dot(lora_scr[:, 2 * LORA_PAD:], g2_ref[...])
    kk = k * k_k
    nrm = jnp.sqrt(_dot2_exact_rhs(kk * kk, seg))
    kk = kk / jnp.maximum(nrm, 1e-12)
    k2 = k * (1.0 + (a - 1.0) * k_a)
    bonus = _dot2_exact_rhs(r * k2 * r_k, seg) * v
    r_out[0] = r
    lw_out[0] = lw
    k_out[0] = k2
    v_out[0] = v
    kk_out[0] = kk
    a_out[0] = a
    g_out[0] = g
    bonus_out[0] = bonus


def _rwkv_in(x, sc, sh, mu, wl1, w_r, w_k, w_v, w2, a2, g2, vec, seg, tm=256, tn=512):
    b, s, d = x.shape
    sub = 8
    vecs = pl.BlockSpec((1, 1, d), lambda bi, i, j: (bi, 0, 0))
    col = lambda rows: pl.BlockSpec((rows, tn), lambda bi, i, j: (0, j))
    out = pl.BlockSpec((1, tm, tn), lambda bi, i, j: (bi, i, j))
    return pl.pallas_call(
        _rwkv_in_kernel,
        grid=(b, s // tm, d // tn),
        in_specs=[pl.BlockSpec((1, tm, d), lambda bi, i, j: (bi, i, 0)),
                  pl.BlockSpec((1, sub, d), lambda bi, i, j: (bi, jnp.maximum(i * (tm // sub) - 1, 0), 0)),
                  vecs, vecs,
                  pl.BlockSpec(mu.shape, lambda bi, i, j: (0, 0)),
                  pl.BlockSpec(wl1.shape, lambda bi, i, j: (0, 0)),
                  col(d), col(d), col(d), col(LORA_PAD), col(LORA_PAD), col(R_GATE_LORA),
                  col(8), pl.BlockSpec(seg.shape, lambda bi, i, j: (0, 0))],
        out_specs=[out] * 8,
        out_shape=[jax.ShapeDtypeStruct((b, s, d), F32)] * 8,
        scratch_shapes=[pltpu.VMEM((6, tm, d), BF16), pltpu.VMEM((tm, 2 * LORA_PAD + R_GATE_LORA), BF16)],
        compiler_params=_params("parallel", "parallel", "arbitrary"),
    )(x, x, sc, sh, mu, wl1, w_r, w_k, w_v, w2, a2, g2, vec, seg)


def _rwkv_rec_kernel(r_ref, lw_ref, k_ref, v_ref, kk_ref, a_ref, o_ref, s_scr, *, n_chunks, n_heads):
    t = pl.program_id(2)

    @pl.when(t == 0)
    def _():
        s_scr[...] = jnp.zeros_like(s_scr)

    c_len = CHUNK
    n = R_HEAD
    ri = lax.broadcasted_iota(jnp.int32, (c_len, c_len), 0)
    ci = lax.broadcasted_iota(jnp.int32, (c_len, c_len), 1)
    strict = ri > ci
    incl = ri >= ci
    tri = jnp.where(incl, 1.0, 0.0).astype(BF16)
    for c in range(n_chunks):
        sl = slice(c * c_len, (c + 1) * c_len)
        lw = lw_ref[0, sl, :]
        r = r_ref[0, sl, :]
        k = k_ref[0, sl, :]
        v = v_ref[0, sl, :]
        kk = kk_ref[0, sl, :]
        kb = kk * a_ref[0, sl, :]
        cum = _cumsum_rows(tri, lw)
        cum_c = cum[c_len - 1:c_len, :]
        r_t = r * jnp.exp(cum)
        a_t = -kk * jnp.exp(cum - lw)
        p_inv = jnp.exp(-cum)
        b_t = kb * p_inv
        k_t = k * p_inv
        p_hat = jnp.exp(cum_c - cum)
        b_h = kb * p_hat
        k_h = k * p_hat
        p_c = jnp.exp(cum_c)
        outs = []
        for h in range(n_heads):
            ls = slice(h * n, (h + 1) * n)
            state = s_scr[h]
            ar = jnp.concatenate([a_t[:, ls], r_t[:, ls]], axis=0)
            bk = jnp.concatenate([b_t[:, ls], k_t[:, ls]], axis=0)
            amat = _dot3(ar, bk, NT_DIMS)
            l_ab = jnp.where(strict, amat[:c_len, :c_len], 0.0)
            l_ak = jnp.where(strict, amat[:c_len, c_len:], 0.0)
            m_rb = jnp.where(incl, amat[c_len:, :c_len], 0.0)
            m_rk = jnp.where(incl, amat[c_len:, c_len:], 0.0)
            vh = v[:, ls]
            lv = _dot3(jnp.concatenate([l_ak, m_rk], axis=0), vh)
            y = jnp.concatenate([a_t[:, ls], lv[:c_len]], axis=1)
            pw = l_ab
            for it in range(6):
                y = y + _dot3(pw, y)
                if it < 5:
                    pw = _dot3(pw, pw)
            w_t = y[:, :n]
            u = _dot3(w_t, state, NT_DIMS) + y[:, n:]
            o = _dot3(r_t[:, ls], state, NT_DIMS) + _dot3(m_rb, u) + lv[c_len:]
            s_scr[h] = state * p_c[:, ls] + _dot3(u, b_h[:, ls], TN_DIMS) + _dot3(vh, k_h[:, ls], TN_DIMS)
            outs.append(o)
        o_ref[0, sl, :] = jnp.concatenate(outs, axis=1)


def _cumsum_rows(tri, x):
    hi, lo = _split(x)
    lo2 = (x - hi.astype(F32) - lo.astype(F32)).astype(BF16)
    return _dot(tri, hi) + _dot(tri, lo) + _dot(tri, lo2)


def _rwkv_rec(r, lw, k, v, kk, a, tb=256, lanes=256):
    b, s, d = r.shape
    tb = min(tb, s)
    n_heads = lanes // R_HEAD
    kern = functools.partial(_rwkv_rec_kernel, n_chunks=tb // CHUNK, n_heads=n_heads)
    blk = pl.BlockSpec((1, tb, lanes), lambda bi, hi, ti: (bi, ti, hi))
    return pl.pallas_call(
        kern,
        grid=(b, d // lanes, s // tb),
        in_specs=[blk] * 6,
        out_specs=blk,
        out_shape=jax.ShapeDtypeStruct((b, s, d), F32),
        scratch_shapes=[pltpu.VMEM((n_heads, R_HEAD, R_HEAD), F32)],
        compiler_params=_params("parallel", "parallel", "arbitrary"),
    )(r, lw, k, v, kk, a)


def _rwkv_out_kernel(o_ref, g_ref, bonus_ref, x_ref, gate_ref, gng_ref, gnb_ref, seg_ref, wo_ref,
                     out_ref, y_scr):
    d = x_ref.shape[2]
    tn = seg_ref.shape[0]
    seg = seg_ref[...]
    inv_n = 1.0 / R_HEAD
    for jt in range(d // tn):
        sl = slice(jt * tn, (jt + 1) * tn)
        o = o_ref[0, :, sl]
        mu = _dot2_exact_rhs(o, seg) * inv_n
        dv = o - mu
        var = _dot2_exact_rhs(dv * dv, seg) * inv_n
        ln = dv * lax.rsqrt(var + R_GN_EPS)
        y = (ln * gng_ref[:, sl] + gnb_ref[:, sl] + bonus_ref[0, :, sl]) * g_ref[0, :, sl]
        y_scr[:, sl] = y.astype(BF16)
    out_ref[0] = x_ref[0] + gate_ref[0] * _dot(y_scr[...], wo_ref[...])


def _rwkv_out(o, g, bonus, x, gate, gng, gnb, seg, w_o, tm=256):
    b, s, d = x.shape
    tok = pl.BlockSpec((1, tm, d), lambda bi, i: (bi, i, 0))
    full = lambda a: pl.BlockSpec(a.shape, lambda bi, i: (0,) * a.ndim)
    return pl.pallas_call(
        _rwkv_out_kernel,
        grid=(b, s // tm),
        in_specs=[tok, tok, tok, tok, pl.BlockSpec((1, 1, d), lambda bi, i: (bi, 0, 0)),
                  full(gng), full(gnb), full(seg), full(w_o)],
        out_specs=tok,
        out_shape=jax.ShapeDtypeStruct((b, s, d), F32),
        scratch_shapes=[pltpu.VMEM((tm, d), BF16)],
        compiler_params=_params("parallel", "parallel"),
    )(o, g, bonus, x, gate, gng, gnb, seg, w_o)


def _rwkv_mixer(x, sc, sh, gate, mu, w_r, w_k, w_v, w_o, w0, w_w1, w_w2, a0, w_a1, w_a2, w_g1, w_g2,
                k_k, k_a, r_k, gn_g, gn_b):
    d = x.shape[2]
    tn = 512
    lora = w_w1.shape[1]
    padc = lambda w: jnp.pad(w, ((0, 0), (0, LORA_PAD - lora)))
    padr = lambda w: jnp.pad(w, ((0, LORA_PAD - lora), (0, 0)))
    wl1 = jnp.concatenate([padc(w_w1), padc(w_a1), w_g1], axis=1).astype(BF16)
    vec = jnp.stack([w0, a0, k_k, k_a, r_k.reshape(-1), w0 * 0, w0 * 0, w0 * 0])
    idx = jnp.arange(tn) // R_HEAD
    seg = (idx[:, None] == idx[None, :]).astype(BF16)
    r, lw, k2, v, kk, a, g, bonus = _rwkv_in(
        x, sc, sh, mu, wl1, w_r.astype(BF16), w_k.astype(BF16), w_v.astype(BF16),
        padr(w_w2).astype(BF16), padr(w_a2).astype(BF16), w_g2.astype(BF16), vec, seg, tn=tn)
    o = _rwkv_rec(r, lw, k2, v, kk, a)
    return _rwkv_out(o, g, bonus, x, gate, gn_g.reshape(1, d), gn_b.reshape(1, d), seg, w_o.astype(BF16))


def kernel(x, c, positions, ada_w, ada_b, mlp_w1, mlp_w2, final_g, a_w_in, a_q_norm_g, a_kv_norm_g, a_w_uq,
           a_w_qidx, a_kidx_ln_g, a_kidx_ln_b, a_w_uk, a_w_uv, a_w_o, b_mu, b_w_r, b_w_k, b_w_v, b_w_o, b_w0,
           b_w_w1, b_w_w2, b_a0, b_w_a1, b_w_a2, b_w_g1, b_w_g2, b_k_k, b_k_a, b_r_k, b_gn_g, b_gn_b):
    depth = ada_w.shape[0]
    d = x.shape[2]
    mod = _ada_mod(c, ada_w, ada_b)
    for i in range(depth):
        sh1, sc1, g1, sh2, sc2, g2 = [mod[i, :, None, m * d:(m + 1) * d] for m in range(6)]
        j = i // 2
        if i % 2 == 0:
            x = _dsa_mixer(x, sc1, sh1, g1, positions, a_w_in[j], a_q_norm_g[j], a_kv_norm_g[j], a_w_uq[j],
                           a_w_qidx[j], a_kidx_ln_g[j], a_kidx_ln_b[j], a_w_uk[j], a_w_uv[j], a_w_o[j])
        else:
            x = _rwkv_mixer(x, sc1, sh1, g1, b_mu[j], b_w_r[j], b_w_k[j], b_w_v[j], b_w_o[j], b_w0[j],
                            b_w_w1[j], b_w_w2[j], b_a0[j], b_w_a1[j], b_w_a2[j], b_w_g1[j], b_w_g2[j],
                            b_k_k[j], b_k_a[j], b_r_k[j], b_gn_g[j], b_gn_b[j])
        x = _mlp(x, sc2, sh2, g2, mlp_w1[i].astype(BF16), mlp_w2[i].astype(BF16), final_g,
                 final_norm=(i == depth - 1))
    return x
```

```python
import functools

import jax
import jax.numpy as jnp
from jax import lax
from jax.experimental import pallas as pl
from jax.experimental.pallas import tpu as pltpu

F32 = jnp.float32
BF16 = jnp.bfloat16

D_MODEL = 2048
DEPTH = 2
CHUNK = 64
EPS = 1e-6
A_HEADS = 16
A_NOPE = 128
A_ROPE = 64
A_V = 128
A_Q_RANK = 512
A_KV_RANK = 256
A_KEY = A_KV_RANK + A_ROPE
IDX_HEADS = 16
IDX_DIM = 64
IDX_ROPE = 32
TOPK_MAX = 256
Q_BLOCK = 128
ROPE_BASE = 10000.0
A_SCALE = (A_NOPE + A_ROPE) ** -0.5
A_IN = A_Q_RANK + A_KV_RANK + A_ROPE + IDX_DIM + IDX_HEADS
A_IN_PAD = 1024
R_HEAD = 64
R_GN_EPS = R_HEAD * 1e-5
LORA_PAD = 128
R_GATE_LORA = 256

VMEM_LIMIT_BYTES = 56 * 1024 * 1024
MASK_VALUE = -1e30

NT_DIMS = (((1,), (1,)), ((), ()))
TN_DIMS = (((0,), (0,)), ((), ()))
NN_DIMS = (((1,), (0,)), ((), ()))

PASSES_PAIR = 1
PASSES_SOLVE = 1
PASSES_STATE = 1


def _params(*sem):
    return pltpu.CompilerParams(dimension_semantics=sem, vmem_limit_bytes=VMEM_LIMIT_BYTES)


def _dot(a, b, dims=NN_DIMS):
    return lax.dot_general(a, b, dims, preferred_element_type=F32)


def _split(x):
    hi = x.astype(BF16)
    lo = (x - hi.astype(F32)).astype(BF16)
    return hi, lo


def _dot3(a, b, dims=NN_DIMS):
    ah, al = _split(a)
    bh, bl = _split(b)
    return _dot(ah, bh, dims) + _dot(ah, bl, dims) + _dot(al, bh, dims)


def _dot2_exact_rhs(a, b_bf16, dims=NN_DIMS):
    ah, al = _split(a)
    return _dot(ah, b_bf16, dims) + _dot(al, b_bf16, dims)


def _rms(x):
    return x * lax.rsqrt(jnp.mean(x * x, axis=-1, keepdims=True) + EPS)


def _rms_mod(x, sc, sh):
    return _rms(x) * (1.0 + sc) + sh


def _sigmoid(x):
    return 1.0 / (1.0 + jnp.exp(-x))


def _rope_half(x, c, s):
    d = x.shape[-1] // 2
    x1 = x[:, :d]
    x2 = x[:, d:]
    return jnp.concatenate([x1 * c - x2 * s, x2 * c + x1 * s], axis=-1)


def _mod_kernel(c_ref, w_ref, b_ref, o_ref):
    c = c_ref[...]
    a = (c * _sigmoid(c)).astype(BF16)
    o_ref[0] = _dot(a, w_ref[0].astype(BF16)) + b_ref[0]


def _ada_mod(c, ada_w, ada_b):
    depth, d, n = ada_w.shape
    b = c.shape[0]
    tn = 1024
    return pl.pallas_call(
        _mod_kernel,
        grid=(depth, n // tn),
        in_specs=[
            pl.BlockSpec((b, d), lambda l, j: (0, 0)),
            pl.BlockSpec((1, d, tn), lambda l, j: (l, 0, j)),
            pl.BlockSpec((1, 1, tn), lambda l, j: (l, 0, j)),
        ],
        out_specs=pl.BlockSpec((1, b, tn), lambda l, j: (l, 0, j)),
        out_shape=jax.ShapeDtypeStruct((depth, b, n), F32),
        compiler_params=_params("arbitrary", "arbitrary"),
    )(c, ada_w, ada_b.reshape(depth, 1, n))


def _mlp_kernel(x_ref, sc_ref, sh_ref, g_ref, w1_ref, w2_ref, fg_ref, o_ref, h_scr, acc_scr, *, final_norm):
    f = pl.program_id(2)

    @pl.when(f == 0)
    def _():
        h_scr[...] = _rms_mod(x_ref[0], sc_ref[0], sh_ref[0]).astype(BF16)
        acc_scr[...] = jnp.zeros_like(acc_scr)

    a = jnp.maximum(_dot(h_scr[...], w1_ref[...]), 0.0)
    acc_scr[...] += _dot((a * a).astype(BF16), w2_ref[...])

    @pl.when(f == pl.num_programs(2) - 1)
    def _():
        y = x_ref[0] + g_ref[0] * acc_scr[...]
        if final_norm:
            y = _rms(y) * fg_ref[...]
        o_ref[0] = y


def _mlp(x, sc, sh, g, w1, w2, final_g, final_norm, tm=512, tf=1024):
    b, s, d = x.shape
    dff = w1.shape[1]
    kern = functools.partial(_mlp_kernel, final_norm=final_norm)
    vec = pl.BlockSpec((1, 1, d), lambda bi, i, f: (bi, 0, 0))
    return pl.pallas_call(
        kern,
        grid=(b, s // tm, dff // tf),
        in_specs=[
            pl.BlockSpec((1, tm, d), lambda bi, i, f: (bi, i, 0)),
            vec, vec, vec,
            pl.BlockSpec((d, tf), lambda bi, i, f: (0, f)),
            pl.BlockSpec((tf, d), lambda bi, i, f: (f, 0)),
            pl.BlockSpec((1, d), lambda bi, i, f: (0, 0)),
        ],
        out_specs=pl.BlockSpec((1, tm, d), lambda bi, i, f: (bi, i, 0)),
        out_shape=jax.ShapeDtypeStruct((b, s, d), F32),
        scratch_shapes=[pltpu.VMEM((tm, d), BF16), pltpu.VMEM((tm, d), F32)],
        compiler_params=_params("parallel", "parallel", "arbitrary"),
    )(x, sc, sh, g, w1, w2, final_g.reshape(1, d))


def _dsa_in_kernel(x_ref, sc_ref, sh_ref, w_ref, qg_ref, kvg_ref, lng_ref, lnb_ref,
                   ca_ref, sa_ref, ci_ref, si_ref, cq_ref, keys_ref, kidx_ref, widx_ref):
    h = _rms_mod(x_ref[0], sc_ref[0], sh_ref[0]).astype(BF16)
    proj = _dot(h, w_ref[...])
    o_kv = A_Q_RANK
    o_kr = o_kv + A_KV_RANK
    o_ki = o_kr + A_ROPE
    o_wi = o_ki + IDX_DIM
    cq_ref[0] = (_rms(proj[:, :o_kv]) * qg_ref[...]).astype(BF16)
    keys_ref[0, :, :A_KV_RANK] = (_rms(proj[:, o_kv:o_kr]) * kvg_ref[...]).astype(BF16)
    keys_ref[0, :, A_KV_RANK:] = _rope_half(proj[:, o_kr:o_ki], ca_ref[0], sa_ref[0]).astype(BF16)
    ki = proj[:, o_ki:o_wi]
    mu = jnp.mean(ki, axis=-1, keepdims=True)
    kc = ki - mu
    var = jnp.mean(kc * kc, axis=-1, keepdims=True)
    ki = kc * lax.rsqrt(var + EPS) * lng_ref[...] + lnb_ref[...]
    ki = jnp.concatenate([_rope_half(ki[:, :IDX_ROPE], ci_ref[0], si_ref[0]), ki[:, IDX_ROPE:]], axis=-1)
    kidx_ref[0] = ki.astype(BF16)
    widx_ref[0] = proj[:, o_wi:o_wi + IDX_HEADS] * (IDX_HEADS ** -0.5 * IDX_DIM ** -0.5)


def _dsa_in(x, sc, sh, w_in_p, qg, kvg, lng, lnb, cos_a, sin_a, cos_i, sin_i, tm=512):
    b, s, d = x.shape
    vec = pl.BlockSpec((1, 1, d), lambda bi, i: (bi, 0, 0))

    def full(a):
        return pl.BlockSpec(a.shape, lambda bi, i: (0,) * a.ndim)

    def tok(w):
        return pl.BlockSpec((1, tm, w), lambda bi, i: (bi, i, 0))

    return pl.pallas_call(
        _dsa_in_kernel,
        grid=(b, s // tm),
        in_specs=[tok(d), vec, vec, full(w_in_p), full(qg), full(kvg), full(lng), full(lnb),
                  tok(A_ROPE // 2), tok(A_ROPE // 2), tok(IDX_ROPE // 2), tok(IDX_ROPE // 2)],
        out_specs=[tok(A_Q_RANK), tok(A_KEY), tok(IDX_DIM), tok(IDX_HEADS)],
        out_shape=[jax.ShapeDtypeStruct((b, s, A_Q_RANK), BF16),
                   jax.ShapeDtypeStruct((b, s, A_KEY), BF16),
                   jax.ShapeDtypeStruct((b, s, IDX_DIM), BF16),
                   jax.ShapeDtypeStruct((b, s, IDX_HEADS), F32)],
        compiler_params=_params("parallel", "parallel"),
    )(x, sc, sh, w_in_p, qg, kvg, lng, lnb, cos_a, sin_a, cos_i, sin_i)


def _fold_kernel(a_ref, b_ref, o_ref):
    o_ref[0] = _dot3(a_ref[0], b_ref[0]).astype(BF16)


def _fold_uq_uk(w_uq_nope, w_uk_t):
    h, rq, dn = w_uq_nope.shape
    rkv = w_uk_t.shape[2]
    return pl.pallas_call(
        _fold_kernel,
        grid=(h,),
        in_specs=[pl.BlockSpec((1, rq, dn), lambda i: (i, 0, 0)),
                  pl.BlockSpec((1, dn, rkv), lambda i: (i, 0, 0))],
        out_specs=pl.BlockSpec((1, rq, rkv), lambda i: (i, 0, 0)),
        out_shape=jax.ShapeDtypeStruct((h, rq, rkv), BF16),
        compiler_params=_params("arbitrary"),
    )(w_uq_nope, w_uk_t)


def _dsa_q_kernel(cq_ref, w_ref, cq_t, sq_t, ci_t, si_t, qatt_ref, qidx_ref):
    cq = cq_ref[0]
    tm = cq.shape[0]
    nb = tm // Q_BLOCK
    n_lat = A_HEADS * A_KV_RANK
    n_rope = A_HEADS * A_ROPE
    n_idx = IDX_HEADS * IDX_DIM
    for h in range(A_HEADS):
        lat = _dot(cq, w_ref[:, h * A_KV_RANK:(h + 1) * A_KV_RANK]) * A_SCALE
        qatt_ref[0, :, h, :, :A_KV_RANK] = lat.astype(BF16).reshape(nb, Q_BLOCK, A_KV_RANK)
    o = n_lat
    r = _dot(cq, w_ref[:, o:o + n_rope])
    rs = _dot(cq, w_ref[:, o + n_rope:o + 2 * n_rope])
    rep = n_rope // cq_t.shape[2]
    qr = ((r * jnp.tile(cq_t[0], (1, rep)) + rs * jnp.tile(sq_t[0], (1, rep))) * A_SCALE).astype(BF16)
    for h in range(A_HEADS):
        qatt_ref[0, :, h, :, A_KV_RANK:] = qr[:, h * A_ROPE:(h + 1) * A_ROPE].reshape(nb, Q_BLOCK, A_ROPE)
    o = n_lat + 2 * n_rope
    qi = _dot(cq, w_ref[:, o:o + n_idx])
    qis = _dot(cq, w_ref[:, o + n_idx:o + 2 * n_idx])
    rep = n_idx // ci_t.shape[2]
    qidx_ref[0] = (qi * jnp.tile(ci_t[0], (1, rep)) + qis * jnp.tile(si_t[0], (1, rep))).astype(BF16)


def _dsa_q(cq, w_q, cq_t, sq_t, ci_t, si_t, tm=512):
    b, s, rq = cq.shape
    nb = tm // Q_BLOCK

    def tok(w):
        return pl.BlockSpec((1, tm, w), lambda bi, i: (bi, i, 0))

    return pl.pallas_call(
        _dsa_q_kernel,
        grid=(b, s // tm),
        in_specs=[tok(rq), pl.BlockSpec(w_q.shape, lambda bi, i: (0, 0)),
                  tok(128), tok(128), tok(128), tok(128)],
        out_specs=[pl.BlockSpec((1, nb, A_HEADS, Q_BLOCK, A_KEY), lambda bi, i: (bi, i, 0, 0, 0)),
                   tok(IDX_HEADS * IDX_DIM)],
        out_shape=[jax.ShapeDtypeStruct((b, s // Q_BLOCK, A_HEADS, Q_BLOCK, A_KEY), BF16),
                   jax.ShapeDtypeStruct((b, s, IDX_HEADS * IDX_DIM), BF16)],
        compiler_params=_params("parallel", "parallel"),
    )(cq, w_q, cq_t, sq_t, ci_t, si_t)


KEY_NEG_INF = -2139095041


def _key_to_float(key):
    return pltpu.bitcast(key ^ ((key >> 31) & jnp.int32(0x7FFFFFFF)), F32)


def _dsa_idx_kernel(qidx_ref, kidx_ref, widx_ref, bias_ref, score_scr, *, topk, tk):
    qb = pl.program_id(1)
    s = kidx_ref.shape[1]
    n_tiles = ((qb + 1) * Q_BLOCK + tk - 1) // tk
    w = widx_ref[0]
    row_chunk = (qb * Q_BLOCK + lax.broadcasted_iota(jnp.int32, (Q_BLOCK, tk), 0)) // CHUNK
    col0 = lax.broadcasted_iota(jnp.int32, (Q_BLOCK, tk), 1)

    def allowed_tile(off):
        return ((col0 + off) // CHUNK) <= row_chunk

    def score_tile(t, carry):
        off = pl.multiple_of(t * tk, tk)
        kt = kidx_ref[0, pl.ds(off, tk), :]
        sc = jnp.zeros((Q_BLOCK, tk), F32)
        for h in range(IDX_HEADS):
            rel = _dot(qidx_ref[0, :, h * IDX_DIM:(h + 1) * IDX_DIM], kt, NT_DIMS)
            sc = sc + jnp.maximum(rel, 0.0) * w[:, h:h + 1]
        score_scr[:, pl.ds(off, tk)] = jnp.where(allowed_tile(off), sc, -jnp.inf)
        return carry

    lax.fori_loop(0, n_tiles, score_tile, 0)

    def count_ge(cand_f):
        def body(t, acc):
            off = pl.multiple_of(t * tk, tk)
            hit = jnp.where(score_scr[:, pl.ds(off, tk)] >= cand_f, 1.0, 0.0)
            for j in range(tk // 128):
                acc = acc + hit[:, j * 128:(j + 1) * 128]
            return acc

        acc = lax.fori_loop(0, n_tiles, body, jnp.zeros((Q_BLOCK, 128), F32))
        return jnp.sum(acc, axis=1, keepdims=True)

    kf = jnp.float32(topk)
    n_visited = (n_tiles * tk).astype(F32)

    def count_key(cand):
        return jnp.where(cand < KEY_NEG_INF, n_visited, count_ge(_key_to_float(cand)))

    thr = jnp.where(count_key(jnp.zeros((Q_BLOCK, 1), jnp.int32)) >= kf, jnp.int32(0), jnp.int32(-2 ** 31))

    def bit_step(i, thr):
        cand = thr | lax.shift_left(jnp.int32(1), jnp.int32(30) - i)
        return jnp.where(count_key(cand) >= kf, cand, thr)

    thr = lax.fori_loop(0, 31, bit_step, thr)
    thr_f = _key_to_float(thr)
    take_all = thr < KEY_NEG_INF

    def write_tile(t, carry):
        off = pl.multiple_of(t * tk, tk)
        sel = ((score_scr[:, pl.ds(off, tk)] >= thr_f) | take_all) & allowed_tile(off)
        bias_ref[0, :, pl.ds(off, tk)] = jnp.where(sel, 0.0, MASK_VALUE).astype(BF16)
        return carry

    lax.fori_loop(0, n_tiles, write_tile, 0)

    def mask_tile(t, carry):
        off = pl.multiple_of(t * tk, tk)
        bias_ref[0, :, pl.ds(off, tk)] = jnp.full((Q_BLOCK, tk), MASK_VALUE, BF16)
        return carry

    lax.fori_loop(n_tiles, s // tk, mask_tile, 0)


def _dsa_idx(qidx, kidx, widx, topk, tk=512):
    b, s, _ = qidx.shape
    tk = min(tk, s)
    kern = functools.partial(_dsa_idx_kernel, topk=topk, tk=tk)
    return pl.pallas_call(
        kern,
        grid=(b, s // Q_BLOCK),
        in_specs=[pl.BlockSpec((1, Q_BLOCK, IDX_HEADS * IDX_DIM), lambda bi, i: (bi, i, 0)),
                  pl.BlockSpec((1, s, IDX_DIM), lambda bi, i: (bi, 0, 0)),
                  pl.BlockSpec((1, Q_BLOCK, IDX_HEADS), lambda bi, i: (bi, i, 0))],
        out_specs=pl.BlockSpec((1, Q_BLOCK, s), lambda bi, i: (bi, i, 0)),
        out_shape=jax.ShapeDtypeStruct((b, s, s), BF16),
        scratch_shapes=[pltpu.VMEM((Q_BLOCK, s), F32)],
        compiler_params=_params("parallel", "parallel"),
    )(qidx, kidx, widx)


def _dsa_attn_kernel(q_ref, keys_ref, bias_ref, o_ref, m_scr, l_scr, acc_scr, *, tk):
    qb = pl.program_id(1)
    rows = A_HEADS * Q_BLOCK
    q = q_ref[0, 0].reshape(rows, A_KEY)
    m_scr[...] = jnp.full_like(m_scr, -jnp.inf)
    l_scr[...] = jnp.zeros_like(l_scr)
    acc_scr[...] = jnp.zeros_like(acc_scr)
    n_tiles = ((qb + 1) * Q_BLOCK + tk - 1) // tk

    def body(t, carry):
        off = pl.multiple_of(t * tk, tk)
        kt = keys_ref[0, pl.ds(off, tk), :]
        sc = _dot(q, kt, NT_DIMS)
        bias = bias_ref[0, :, pl.ds(off, tk)].astype(F32)
        sc = (sc.reshape(A_HEADS, Q_BLOCK, tk) + bias[None]).reshape(rows, tk)
        m_prev = m_scr[...]
        m_new = jnp.maximum(m_prev, jnp.max(sc, axis=1, keepdims=True))
        alpha = jnp.exp(m_prev - m_new)
        p = jnp.exp(sc - jnp.tile(m_new, (1, tk // 128)))
        l_scr[...] = alpha * l_scr[...] + jnp.sum(p, axis=1, keepdims=True)
        acc_scr[...] = (acc_scr[...] * jnp.tile(alpha, (1, A_KV_RANK // 128))
                        + _dot(p.astype(BF16), kt[:, :A_KV_RANK]))
        m_scr[...] = m_new
        return carry

    lax.fori_loop(0, n_tiles, body, 0)
    o = acc_scr[...] / jnp.tile(l_scr[...], (1, A_KV_RANK // 128))
    o_ref[0, 0] = o.astype(BF16).reshape(A_HEADS, Q_BLOCK, A_KV_RANK)


def _dsa_attn(qatt, keys, bias, tk=512):
    b, nb = qatt.shape[:2]
    s = keys.shape[1]
    tk = min(tk, s)
    rows = A_HEADS * Q_BLOCK
    kern = functools.partial(_dsa_attn_kernel, tk=tk)
    return pl.pallas_call(
        kern,
        grid=(b, nb),
        in_specs=[pl.BlockSpec((1, 1, A_HEADS, Q_BLOCK, A_KEY), lambda bi, i: (bi, i, 0, 0, 0)),
                  pl.BlockSpec((1, s, A_KEY), lambda bi, i: (bi, 0, 0)),
                  pl.BlockSpec((1, Q_BLOCK, s), lambda bi, i: (bi, i, 0))],
        out_specs=pl.BlockSpec((1, 1, A_HEADS, Q_BLOCK, A_KV_RANK), lambda bi, i: (bi, i, 0, 0, 0)),
        out_shape=jax.ShapeDtypeStruct((b, nb, A_HEADS, Q_BLOCK, A_KV_RANK), BF16),
        scratch_shapes=[pltpu.VMEM((rows, 128), F32), pltpu.VMEM((rows, 128), F32),
                        pltpu.VMEM((rows, A_KV_RANK), F32)],
        compiler_params=_params("parallel", "parallel"),
    )(qatt, keys, bias)


def _dsa_out_kernel(ol_ref, wuv_ref, wo_ref, x_ref, g_ref, o_ref, y_scr):
    nb = ol_ref.shape[1]
    tm = nb * Q_BLOCK
    for h in range(A_HEADS):
        oh = ol_ref[0, :, h].reshape(tm, A_KV_RANK)
        y_scr[:, h * A_V:(h + 1) * A_V] = _dot(oh, wuv_ref[h]).astype(BF16)
    o_ref[0] = x_ref[0] + g_ref[0] * _dot(y_scr[...], wo_ref[...])


def _dsa_out(olat, w_uv_h, w_o, x, g, tm=512):
    b, s, d = x.shape
    nb = tm // Q_BLOCK
    return pl.pallas_call(
        _dsa_out_kernel,
        grid=(b, s // tm),
        in_specs=[pl.BlockSpec((1, nb, A_HEADS, Q_BLOCK, A_KV_RANK), lambda bi, i: (bi, i, 0, 0, 0)),
                  pl.BlockSpec(w_uv_h.shape, lambda bi, i: (0, 0, 0)),
                  pl.BlockSpec(w_o.shape, lambda bi, i: (0, 0)),
                  pl.BlockSpec((1, tm, d), lambda bi, i: (bi, i, 0)),
                  pl.BlockSpec((1, 1, d), lambda bi, i: (bi, 0, 0))],
        out_specs=pl.BlockSpec((1, tm, d), lambda bi, i: (bi, i, 0)),
        out_shape=jax.ShapeDtypeStruct((b, s, d), F32),
        scratch_shapes=[pltpu.VMEM((tm, A_HEADS * A_V), BF16)],
        compiler_params=_params("parallel", "parallel"),
    )(olat, w_uv_h, w_o, x, g)


def _rope_tables(positions):
    def angles(dim):
        inv = 1.0 / (ROPE_BASE ** (jnp.arange(0, dim, 2, dtype=F32) / dim))
        ang = positions.astype(F32)[..., None] * inv
        return jnp.cos(ang), jnp.sin(ang)

    cos_a, sin_a = angles(A_ROPE)
    cos_i, sin_i = angles(IDX_ROPE)
    return cos_a, sin_a, cos_i, sin_i


def _dsa_mixer(x, sc, sh, g, positions, w_in, qg, kvg, w_uq, w_qidx, lng, lnb, w_uk, w_uv, w_o):
    b, s, d = x.shape
    topk = min(TOPK_MAX, s // 4)
    cos_a, sin_a, cos_i, sin_i = _rope_tables(positions)
    w_in_p = jnp.pad(w_in, ((0, 0), (0, A_IN_PAD - A_IN))).astype(BF16)
    cq, keys, kidx, widx = _dsa_in(x, sc, sh, w_in_p, qg.reshape(1, -1), kvg.reshape(1, -1),
                                   lng.reshape(1, -1), lnb.reshape(1, -1), cos_a, sin_a, cos_i, sin_i)

    rq = w_uq.shape[0]
    w_uq_h = w_uq.reshape(rq, A_HEADS, A_NOPE + A_ROPE)
    w_lat = _fold_uq_uk(jnp.transpose(w_uq_h[:, :, :A_NOPE], (1, 0, 2)), jnp.transpose(w_uk, (1, 2, 0)))
    w_lat = jnp.transpose(w_lat, (1, 0, 2)).reshape(rq, A_HEADS * A_KV_RANK)
    w_r = w_uq_h[:, :, A_NOPE:]
    w_rs = jnp.concatenate([w_r[..., A_ROPE // 2:], w_r[..., :A_ROPE // 2]], axis=-1)
    w_i = w_qidx.reshape(rq, IDX_HEADS, IDX_DIM)
    hr = IDX_ROPE // 2
    w_is = jnp.concatenate([w_i[..., hr:IDX_ROPE], w_i[..., :hr], jnp.zeros_like(w_i[..., IDX_ROPE:])], axis=-1)
    w_q = jnp.concatenate([w_lat, w_r.reshape(rq, -1).astype(BF16), w_rs.reshape(rq, -1).astype(BF16),
                           w_i.reshape(rq, -1).astype(BF16), w_is.reshape(rq, -1).astype(BF16)], axis=1)
    cq_t = jnp.concatenate([cos_a, cos_a, cos_a, cos_a], axis=-1)
    sq_t = jnp.concatenate([-sin_a, sin_a, -sin_a, sin_a], axis=-1)
    one = jnp.ones(cos_i.shape[:-1] + (IDX_DIM - IDX_ROPE,), F32)
    ci_t = jnp.concatenate([cos_i, cos_i, one, cos_i, cos_i, one], axis=-1)
    si_t = jnp.concatenate([-sin_i, sin_i, 0 * one, -sin_i, sin_i, 0 * one], axis=-1)
    qatt, qidx = _dsa_q(cq, w_q, cq_t, sq_t, ci_t, si_t)

    bias = _dsa_idx(qidx, kidx, widx, topk)
    olat = _dsa_attn(qatt, keys, bias)
    w_uv_h = jnp.transpose(w_uv, (1, 0, 2)).astype(BF16)
    return _dsa_out(olat, w_uv_h, w_o.astype(BF16), x, g)


def _rwkv_in_kernel(x_ref, xp_ref, sc_ref, sh_ref, mu_ref, wl1_ref, wr_ref, wk_ref, wv_ref,
                    w2_ref, a2_ref, g2_ref, vec_ref, seg_ref,
                    r_out, lw_out, k_out, v_out, kk_out, a_out, g_out, bonus_out,
                    mix_scr, lora_scr):
    i = pl.program_id(1)
    j = pl.program_id(2)

    @pl.when(j == 0)
    def _():
        sc = sc_ref[0]
        sh = sh_ref[0]
        h = _rms_mod(x_ref[0], sc, sh)
        hp = _rms_mod(xp_ref[0, 7:8, :], sc, sh)
        hp = jnp.where(i == 0, 0.0, hp)
        rows = lax.broadcasted_iota(jnp.int32, h.shape, 0)
        h_prev = jnp.where(rows == 0, hp, pltpu.roll(h, 1, axis=0))
        delta = h_prev - h
        for m in range(6):
            mix_scr[m] = (h + delta * mu_ref[m:m + 1, :]).astype(BF16)
        tw = jnp.tanh(_dot(mix_scr[1], wl1_ref[:, :LORA_PAD]))
        ta = _dot(mix_scr[4], wl1_ref[:, LORA_PAD:2 * LORA_PAD])
        tg = _sigmoid(_dot(mix_scr[5], wl1_ref[:, 2 * LORA_PAD:]))
        lora_scr[:, :LORA_PAD] = tw.astype(BF16)
        lora_scr[:, LORA_PAD:2 * LORA_PAD] = ta.astype(BF16)
        lora_scr[:, 2 * LORA_PAD:] = tg.astype(BF16)

    w0 = vec_ref[0:1, :]
    a0 = vec_ref[1:2, :]
    k_k = vec_ref[2:3, :]
    k_a = vec_ref[3:4, :]
    r_k = vec_ref[4:5, :]
    seg = seg_ref[...]
    r = _dot(mix_scr[0], wr_ref[...])
    k = _dot(mix_scr[2], wk_ref[...])
    v = _dot(mix_scr[3], wv_ref[...])
    z = -(w0 + _dot(lora_scr[:, :LORA_PAD], w2_ref[...]))
    softplus = jnp.maximum(z, 0.0) + jnp.log(1.0 + jnp.exp(-jnp.abs(z)))
    lw = -jnp.exp(-softplus - 0.5)
    a = _sigmoid(a0 + _dot(lora_scr[:, LORA_PAD:2 * LORA_PAD], a2_ref[...]))
    g = _dot(lora_scr[:, 2 * LORA_PAD:], g2_ref[...])
    kk = k * k_k
    nrm = jnp.sqrt(_dot2_exact_rhs(kk * kk, seg))
    kk = kk / jnp.maximum(nrm, 1e-12)
    k2 = k * (1.0 + (a - 1.0) * k_a)
    bonus = _dot2_exact_rhs(r * k2 * r_k, seg) * v
    r_out[0] = r.astype(BF16)
    lw_out[0] = lw
    k_out[0] = k2.astype(BF16)
    v_out[0] = v.astype(BF16)
    kk_out[0] = kk.astype(BF16)
    a_out[0] = a.astype(BF16)
    g_out[0] = g.astype(BF16)
    bonus_out[0] = bonus.astype(BF16)


def _rwkv_in(x, sc, sh, mu, wl1, w_r, w_k, w_v, w2, a2, g2, vec, seg, tm=512, tn=256):
    b, s, d = x.shape
    sub = 8
    vecs = pl.BlockSpec((1, 1, d), lambda bi, i, j: (bi, 0, 0))
    col = lambda rows: pl.BlockSpec((rows, tn), lambda bi, i, j: (0, j))
    out = pl.BlockSpec((1, tm, tn), lambda bi, i, j: (bi, i, j))
    return pl.pallas_call(
        _rwkv_in_kernel,
        grid=(b, s // tm, d // tn),
        in_specs=[pl.BlockSpec((1, tm, d), lambda bi, i, j: (bi, i, 0)),
                  pl.BlockSpec((1, sub, d), lambda bi, i, j: (bi, jnp.maximum(i * (tm // sub) - 1, 0), 0)),
                  vecs, vecs,
                  pl.BlockSpec(mu.shape, lambda bi, i, j: (0, 0)),
                  pl.BlockSpec(wl1.shape, lambda bi, i, j: (0, 0)),
                  col(d), col(d), col(d), col(LORA_PAD), col(LORA_PAD), col(R_GATE_LORA),
                  col(8), pl.BlockSpec(seg.shape, lambda bi, i, j: (0, 0))],
        out_specs=[out] * 8,
        out_shape=[jax.ShapeDtypeStruct((b, s, d), F32 if i == 1 else BF16) for i in range(8)],
        scratch_shapes=[pltpu.VMEM((6, tm, d), BF16), pltpu.VMEM((tm, 2 * LORA_PAD + R_GATE_LORA), BF16)],
        compiler_params=_params("parallel", "parallel", "arbitrary"),
    )(x, x, sc, sh, mu, wl1, w_r, w_k, w_v, w2, a2, g2, vec, seg)


def _rwkv_rec_kernel(r_ref, lw_ref, k_ref, v_ref, kk_ref, a_ref, o_ref, s_scr, *, n_chunks, n_heads):
    t = pl.program_id(2)

    @pl.when(t == 0)
    def _():
        s_scr[...] = jnp.zeros_like(s_scr)

    c_len = CHUNK
    n = R_HEAD
    ri = lax.broadcasted_iota(jnp.int32, (c_len, c_len), 0)
    ci = lax.broadcasted_iota(jnp.int32, (c_len, c_len), 1)
    strict = ri > ci
    incl = ri >= ci
    tri = jnp.where(incl, 1.0, 0.0).astype(BF16)
    items = [(c, h) for c in range(n_chunks) for h in range(n_heads)]

    pre = []
    for c in range(n_chunks):
        sl = slice(c * c_len, (c + 1) * c_len)
        lw = lw_ref[0, sl, :]
        k = k_ref[0, sl, :].astype(F32)
        kk = kk_ref[0, sl, :].astype(F32)
        kb = kk * a_ref[0, sl, :].astype(F32)
        cum = _cumsum_rows(tri, lw)
        cum_c = cum[c_len - 1:c_len, :]
        p_inv = jnp.exp(-cum)
        p_hat = jnp.exp(cum_c - cum)
        pre.append(dict(
            r_t=r_ref[0, sl, :].astype(F32) * jnp.exp(cum), a_t=-kk * jnp.exp(cum - lw), b_t=kb * p_inv,
            k_t=k * p_inv, b_h=kb * p_hat, k_h=k * p_hat, p_c=jnp.exp(cum_c), v=v_ref[0, sl, :].astype(F32)))

    def head(c, h, name):
        return pre[c][name][:, h * n:(h + 1) * n]

    amat = {}
    for c, h in items:
        ar = jnp.concatenate([head(c, h, "a_t"), head(c, h, "r_t")], axis=0)
        bk = jnp.concatenate([head(c, h, "b_t"), head(c, h, "k_t")], axis=0)
        amat[c, h] = _mm(ar, bk, NT_DIMS, PASSES_PAIR)
    l_ab, m_rb, lv = {}, {}, {}
    for c, h in items:
        am = amat[c, h]
        l_ab[c, h] = jnp.where(strict, am[:c_len, :c_len], 0.0)
        m_rb[c, h] = jnp.where(incl, am[c_len:, :c_len], 0.0)
        lm = jnp.concatenate([jnp.where(strict, am[:c_len, c_len:], 0.0),
                              jnp.where(incl, am[c_len:, c_len:], 0.0)], axis=0)
        lv[c, h] = _mm(lm, head(c, h, "v"), NN_DIMS, PASSES_PAIR)
    z = {}
    for c, h in items:
        z[c, h] = jnp.concatenate([head(c, h, "a_t"), lv[c, h][:c_len], l_ab[c, h]], axis=1)
    for it in range(6):
        for c, h in items:
            zz = z[c, h]
            pw = zz[:, 2 * n:]
            if it < 5:
                x = _mm(pw, zz, NN_DIMS, PASSES_SOLVE)
                z[c, h] = jnp.concatenate([zz[:, :2 * n] + x[:, :2 * n], x[:, 2 * n:]], axis=1)
            else:
                z[c, h] = zz[:, :2 * n] + _mm(pw, zz[:, :2 * n], NN_DIMS, PASSES_SOLVE)
    state = [s_scr[h] for h in range(n_heads)]
    for c in range(n_chunks):
        u = [_mm(z[c, h][:, :n], state[h], NT_DIMS, PASSES_STATE) + z[c, h][:, n:] for h in range(n_heads)]
        o = [_mm(head(c, h, "r_t"), state[h], NT_DIMS, PASSES_STATE) + _mm(m_rb[c, h], u[h], NN_DIMS, PASSES_STATE)
             + lv[c, h][c_len:] for h in range(n_heads)]
        state = [state[h] * head(c, h, "p_c")
                 + _mm(u[h], head(c, h, "b_h"), TN_DIMS, PASSES_STATE)
                 + _mm(head(c, h, "v"), head(c, h, "k_h"), TN_DIMS, PASSES_STATE) for h in range(n_heads)]
        o_ref[0, c * c_len:(c + 1) * c_len, :] = jnp.concatenate(o, axis=1)
    for h in range(n_heads):
        s_scr[h] = state[h]


def _mm(a, b, dims=NN_DIMS, passes=1):
    if passes == 1:
        return _dot(a.astype(BF16), b.astype(BF16), dims)
    return _dot3(a, b, dims)


def _cumsum_rows(tri, x):
    hi, lo = _split(x)
    lo2 = (x - hi.astype(F32) - lo.astype(F32)).astype(BF16)
    return _dot(tri, hi) + _dot(tri, lo) + _dot(tri, lo2)


def _rwkv_rec(r, lw, k, v, kk, a, tb=256, lanes=256):
    b, s, d = r.shape
    tb = min(tb, s)
    n_heads = lanes // R_HEAD
    kern = functools.partial(_rwkv_rec_kernel, n_chunks=tb // CHUNK, n_heads=n_heads)
    blk = pl.BlockSpec((1, tb, lanes), lambda bi, hi, ti: (bi, ti, hi))
    return pl.pallas_call(
        kern,
        grid=(b, d // lanes, s // tb),
        in_specs=[blk] * 6,
        out_specs=blk,
        out_shape=jax.ShapeDtypeStruct((b, s, d), F32),
        scratch_shapes=[pltpu.VMEM((n_heads, R_HEAD, R_HEAD), F32)],
        compiler_params=_params("parallel", "parallel", "arbitrary"),
    )(r, lw, k, v, kk, a)


def _rwkv_out_kernel(o_ref, g_ref, bonus_ref, x_ref, gate_ref, gng_ref, gnb_ref, seg_ref, wo_ref,
                     out_ref, y_scr):
    d = x_ref.shape[2]
    tn = seg_ref.shape[0]
    seg = seg_ref[...]
    inv_n = 1.0 / R_HEAD
    for jt in range(d // tn):
        sl = slice(jt * tn, (jt + 1) * tn)
        o = o_ref[0, :, sl]
        mu = _dot2_exact_rhs(o, seg) * inv_n
        dv = o - mu
        var = _dot2_exact_rhs(dv * dv, seg) * inv_n
        ln = dv * lax.rsqrt(var + R_GN_EPS)
        y = ((ln * gng_ref[:, sl] + gnb_ref[:, sl] + bonus_ref[0, :, sl].astype(F32))
             * g_ref[0, :, sl].astype(F32))
        y_scr[:, sl] = y.astype(BF16)
    out_ref[0] = x_ref[0] + gate_ref[0] * _dot(y_scr[...], wo_ref[...])


def _rwkv_out(o, g, bonus, x, gate, gng, gnb, seg, w_o, tm=256):
    b, s, d = x.shape
    tok = pl.BlockSpec((1, tm, d), lambda bi, i: (bi, i, 0))
    full = lambda a: pl.BlockSpec(a.shape, lambda bi, i: (0,) * a.ndim)
    return pl.pallas_call(
        _rwkv_out_kernel,
        grid=(b, s // tm),
        in_specs=[tok, tok, tok, tok, pl.BlockSpec((1, 1, d), lambda bi, i: (bi, 0, 0)),
                  full(gng), full(gnb), full(seg), full(w_o)],
        out_specs=tok,
        out_shape=jax.ShapeDtypeStruct((b, s, d), F32),
        scratch_shapes=[pltpu.VMEM((tm, d), BF16)],
        compiler_params=_params("parallel", "parallel"),
    )(o, g, bonus, x, gate, gng, gnb, seg, w_o)


def _rwkv_mixer(x, sc, sh, gate, mu, w_r, w_k, w_v, w_o, w0, w_w1, w_w2, a0, w_a1, w_a2, w_g1, w_g2,
                k_k, k_a, r_k, gn_g, gn_b):
    d = x.shape[2]
    tn = 256
    lora = w_w1.shape[1]
    padc = lambda w: jnp.pad(w, ((0, 0), (0, LORA_PAD - lora)))
    padr = lambda w: jnp.pad(w, ((0, LORA_PAD - lora), (0, 0)))
    wl1 = jnp.concatenate([padc(w_w1), padc(w_a1), w_g1], axis=1).astype(BF16)
    vec = jnp.stack([w0, a0, k_k, k_a, r_k.reshape(-1), w0 * 0, w0 * 0, w0 * 0])
    idx = jnp.arange(tn) // R_HEAD
    seg = (idx[:, None] == idx[None, :]).astype(BF16)
    r, lw, k2, v, kk, a, g, bonus = _rwkv_in(
        x, sc, sh, mu, wl1, w_r.astype(BF16), w_k.astype(BF16), w_v.astype(BF16),
        padr(w_w2).astype(BF16), padr(w_a2).astype(BF16), w_g2.astype(BF16), vec, seg, tn=tn)
    o = _rwkv_rec(r, lw, k2, v, kk, a)
    return _rwkv_out(o, g, bonus, x, gate, gn_g.reshape(1, d), gn_b.reshape(1, d), seg, w_o.astype(BF16))


def kernel(x, c, positions, ada_w, ada_b, mlp_w1, mlp_w2, final_g, a_w_in, a_q_norm_g, a_kv_norm_g, a_w_uq,
           a_w_qidx, a_kidx_ln_g, a_kidx_ln_b, a_w_uk, a_w_uv, a_w_o, b_mu, b_w_r, b_w_k, b_w_v, b_w_o, b_w0,
           b_w_w1, b_w_w2, b_a0, b_w_a1, b_w_a2, b_w_g1, b_w_g2, b_k_k, b_k_a, b_r_k, b_gn_g, b_gn_b):
    depth = ada_w.shape[0]
    d = x.shape[2]
    mod = _ada_mod(c, ada_w, ada_b)
    for i in range(depth):
        sh1, sc1, g1, sh2, sc2, g2 = [mod[i, :, None, m * d:(m + 1) * d] for m in range(6)]
        j = i // 2
        if i % 2 == 0:
            x = _dsa_mixer(x, sc1, sh1, g1, positions, a_w_in[j], a_q_norm_g[j], a_kv_norm_g[j], a_w_uq[j],
                           a_w_qidx[j], a_kidx_ln_g[j], a_kidx_ln_b[j], a_w_uk[j], a_w_uv[j], a_w_o[j])
        else:
            x = _rwkv_mixer(x, sc1, sh1, g1, b_mu[j], b_w_r[j], b_w_k[j], b_w_v[j], b_w_o[j], b_w0[j],
                            b_w_w1[j], b_w_w2[j], b_a0[j], b_w_a1[j], b_w_a2[j], b_w_g1[j], b_w_g2[j],
                            b_k_k[j], b_k_a[j], b_r_k[j], b_gn_g[j], b_gn_b[j])
        x = _mlp(x, sc2, sh2, g2, mlp_w1[i].astype(BF16), mlp_w2[i].astype(BF16), final_g,
                 final_norm=(i == depth - 1))
    return x
```

```python
import functools

import jax
import jax.numpy as jnp
from jax import lax
from jax.experimental import pallas as pl
from jax.experimental.pallas import tpu as pltpu

F32 = jnp.float32
BF16 = jnp.bfloat16

D_MODEL = 2048
DEPTH = 2
CHUNK = 64
EPS = 1e-6
A_HEADS = 16
A_NOPE = 128
A_ROPE = 64
A_V = 128
A_Q_RANK = 512
A_KV_RANK = 256
A_KEY = A_KV_RANK + A_ROPE
IDX_HEADS = 16
IDX_DIM = 64
IDX_ROPE = 32
TOPK_MAX = 256
Q_BLOCK = 128
ROPE_BASE = 10000.0
A_SCALE = (A_NOPE + A_ROPE) ** -0.5
A_IN = A_Q_RANK + A_KV_RANK + A_ROPE + IDX_DIM + IDX_HEADS
A_IN_PAD = 1024
R_HEAD = 64
R_GN_EPS = R_HEAD * 1e-5
LORA_PAD = 128
R_GATE_LORA = 256

VMEM_LIMIT_BYTES = 56 * 1024 * 1024
MASK_VALUE = -1e30

NT_DIMS = (((1,), (1,)), ((), ()))
TN_DIMS = (((0,), (0,)), ((), ()))
NN_DIMS = (((1,), (0,)), ((), ()))

PASSES_PAIR = 1
PASSES_SOLVE = 1
PASSES_STATE = 1


def _params(*sem):
    return pltpu.CompilerParams(dimension_semantics=sem, vmem_limit_bytes=VMEM_LIMIT_BYTES)


def _dot(a, b, dims=NN_DIMS):
    return lax.dot_general(a, b, dims, preferred_element_type=F32)


def _split(x):
    hi = x.astype(BF16)
    lo = (x - hi.astype(F32)).astype(BF16)
    return hi, lo


def _dot3(a, b, dims=NN_DIMS):
    ah, al = _split(a)
    bh, bl = _split(b)
    return _dot(ah, bh, dims) + _dot(ah, bl, dims) + _dot(al, bh, dims)


def _dot2_exact_rhs(a, b_bf16, dims=NN_DIMS):
    ah, al = _split(a)
    return _dot(ah, b_bf16, dims) + _dot(al, b_bf16, dims)


def _rms(x):
    return x * lax.rsqrt(jnp.mean(x * x, axis=-1, keepdims=True) + EPS)


def _rms_mod(x, sc, sh):
    return _rms(x) * (1.0 + sc) + sh


def _sigmoid(x):
    return 1.0 / (1.0 + jnp.exp(-x))


def _rope_half(x, c, s):
    d = x.shape[-1] // 2
    x1 = x[:, :d]
    x2 = x[:, d:]
    return jnp.concatenate([x1 * c - x2 * s, x2 * c + x1 * s], axis=-1)


def _mod_kernel(c_ref, w_ref, b_ref, o_ref):
    c = c_ref[...]
    a = (c * _sigmoid(c)).astype(BF16)
    o_ref[0] = _dot(a, w_ref[0].astype(BF16)) + b_ref[0]


def _ada_mod(c, ada_w, ada_b):
    depth, d, n = ada_w.shape
    b = c.shape[0]
    tn = 1024
    return pl.pallas_call(
        _mod_kernel,
        grid=(depth, n // tn),
        in_specs=[
            pl.BlockSpec((b, d), lambda l, j: (0, 0)),
            pl.BlockSpec((1, d, tn), lambda l, j: (l, 0, j)),
            pl.BlockSpec((1, 1, tn), lambda l, j: (l, 0, j)),
        ],
        out_specs=pl.BlockSpec((1, b, tn), lambda l, j: (l, 0, j)),
        out_shape=jax.ShapeDtypeStruct((depth, b, n), F32),
        compiler_params=_params("arbitrary", "arbitrary"),
    )(c, ada_w, ada_b.reshape(depth, 1, n))


def _mlp_kernel(x_ref, sc_ref, sh_ref, g_ref, w1_ref, w2_ref, fg_ref, o_ref, h_scr, acc_scr, *, final_norm):
    f = pl.program_id(2)

    @pl.when(f == 0)
    def _():
        h_scr[...] = _rms_mod(x_ref[0], sc_ref[0], sh_ref[0]).astype(BF16)
        acc_scr[...] = jnp.zeros_like(acc_scr)

    a = jnp.maximum(_dot(h_scr[...], w1_ref[...]), 0.0)
    acc_scr[...] += _dot((a * a).astype(BF16), w2_ref[...])

    @pl.when(f == pl.num_programs(2) - 1)
    def _():
        y = x_ref[0] + g_ref[0] * acc_scr[...]
        if final_norm:
            y = _rms(y) * fg_ref[...]
        o_ref[0] = y


def _mlp(x, sc, sh, g, w1, w2, final_g, final_norm, tm=512, tf=1024):
    b, s, d = x.shape
    dff = w1.shape[1]
    kern = functools.partial(_mlp_kernel, final_norm=final_norm)
    vec = pl.BlockSpec((1, 1, d), lambda bi, i, f: (bi, 0, 0))
    return pl.pallas_call(
        kern,
        grid=(b, s // tm, dff // tf),
        in_specs=[
            pl.BlockSpec((1, tm, d), lambda bi, i, f: (bi, i, 0)),
            vec, vec, vec,
            pl.BlockSpec((d, tf), lambda bi, i, f: (0, f)),
            pl.BlockSpec((tf, d), lambda bi, i, f: (f, 0)),
            pl.BlockSpec((1, d), lambda bi, i, f: (0, 0)),
        ],
        out_specs=pl.BlockSpec((1, tm, d), lambda bi, i, f: (bi, i, 0)),
        out_shape=jax.ShapeDtypeStruct((b, s, d), F32),
        scratch_shapes=[pltpu.VMEM((tm, d), BF16), pltpu.VMEM((tm, d), F32)],
        compiler_params=_params("parallel", "parallel", "arbitrary"),
    )(x, sc, sh, g, w1, w2, final_g.reshape(1, d))


def _dsa_in_kernel(x_ref, sc_ref, sh_ref, w_ref, qg_ref, kvg_ref, lng_ref, lnb_ref,
                   ca_ref, sa_ref, ci_ref, si_ref, cq_ref, keys_ref, kidx_ref, widx_ref):
    h = _rms_mod(x_ref[0], sc_ref[0], sh_ref[0]).astype(BF16)
    proj = _dot(h, w_ref[...])
    o_kv = A_Q_RANK
    o_kr = o_kv + A_KV_RANK
    o_ki = o_kr + A_ROPE
    o_wi = o_ki + IDX_DIM
    cq_ref[0] = (_rms(proj[:, :o_kv]) * qg_ref[...]).astype(BF16)
    keys_ref[0, :, :A_KV_RANK] = (_rms(proj[:, o_kv:o_kr]) * kvg_ref[...]).astype(BF16)
    keys_ref[0, :, A_KV_RANK:] = _rope_half(proj[:, o_kr:o_ki], ca_ref[0], sa_ref[0]).astype(BF16)
    ki = proj[:, o_ki:o_wi]
    mu = jnp.mean(ki, axis=-1, keepdims=True)
    kc = ki - mu
    var = jnp.mean(kc * kc, axis=-1, keepdims=True)
    ki = kc * lax.rsqrt(var + EPS) * lng_ref[...] + lnb_ref[...]
    ki = jnp.concatenate([_rope_half(ki[:, :IDX_ROPE], ci_ref[0], si_ref[0]), ki[:, IDX_ROPE:]], axis=-1)
    kidx_ref[0] = ki.astype(BF16)
    widx_ref[0] = proj[:, o_wi:o_wi + IDX_HEADS] * (IDX_HEADS ** -0.5 * IDX_DIM ** -0.5)


def _dsa_in(x, sc, sh, w_in_p, qg, kvg, lng, lnb, cos_a, sin_a, cos_i, sin_i, tm=512):
    b, s, d = x.shape
    vec = pl.BlockSpec((1, 1, d), lambda bi, i: (bi, 0, 0))

    def full(a):
        return pl.BlockSpec(a.shape, lambda bi, i: (0,) * a.ndim)

    def tok(w):
        return pl.BlockSpec((1, tm, w), lambda bi, i: (bi, i, 0))

    return pl.pallas_call(
        _dsa_in_kernel,
        grid=(b, s // tm),
        in_specs=[tok(d), vec, vec, full(w_in_p), full(qg), full(kvg), full(lng), full(lnb),
                  tok(A_ROPE // 2), tok(A_ROPE // 2), tok(IDX_ROPE // 2), tok(IDX_ROPE // 2)],
        out_specs=[tok(A_Q_RANK), tok(A_KEY), tok(IDX_DIM), tok(IDX_HEADS)],
        out_shape=[jax.ShapeDtypeStruct((b, s, A_Q_RANK), BF16),
                   jax.ShapeDtypeStruct((b, s, A_KEY), BF16),
                   jax.ShapeDtypeStruct((b, s, IDX_DIM), BF16),
                   jax.ShapeDtypeStruct((b, s, IDX_HEADS), F32)],
        compiler_params=_params("parallel", "parallel"),
    )(x, sc, sh, w_in_p, qg, kvg, lng, lnb, cos_a, sin_a, cos_i, sin_i)


def _fold_kernel(a_ref, b_ref, o_ref):
    o_ref[0] = _dot3(a_ref[0], b_ref[0]).astype(BF16)


def _fold_uq_uk(w_uq_nope, w_uk_t):
    h, rq, dn = w_uq_nope.shape
    rkv = w_uk_t.shape[2]
    return pl.pallas_call(
        _fold_kernel,
        grid=(h,),
        in_specs=[pl.BlockSpec((1, rq, dn), lambda i: (i, 0, 0)),
                  pl.BlockSpec((1, dn, rkv), lambda i: (i, 0, 0))],
        out_specs=pl.BlockSpec((1, rq, rkv), lambda i: (i, 0, 0)),
        out_shape=jax.ShapeDtypeStruct((h, rq, rkv), BF16),
        compiler_params=_params("arbitrary"),
    )(w_uq_nope, w_uk_t)


def _dsa_q_kernel(cq_ref, w_ref, cq_t, sq_t, ci_t, si_t, qatt_ref, qidx_ref):
    cq = cq_ref[0]
    tm = cq.shape[0]
    nb = tm // Q_BLOCK
    n_lat = A_HEADS * A_KV_RANK
    n_rope = A_HEADS * A_ROPE
    n_idx = IDX_HEADS * IDX_DIM
    for h in range(A_HEADS):
        lat = _dot(cq, w_ref[:, h * A_KV_RANK:(h + 1) * A_KV_RANK]) * A_SCALE
        qatt_ref[0, :, h, :, :A_KV_RANK] = lat.astype(BF16).reshape(nb, Q_BLOCK, A_KV_RANK)
    o = n_lat
    r = _dot(cq, w_ref[:, o:o + n_rope])
    rs = _dot(cq, w_ref[:, o + n_rope:o + 2 * n_rope])
    rep = n_rope // cq_t.shape[2]
    qr = ((r * jnp.tile(cq_t[0], (1, rep)) + rs * jnp.tile(sq_t[0], (1, rep))) * A_SCALE).astype(BF16)
    for h in range(A_HEADS):
        qatt_ref[0, :, h, :, A_KV_RANK:] = qr[:, h * A_ROPE:(h + 1) * A_ROPE].reshape(nb, Q_BLOCK, A_ROPE)
    o = n_lat + 2 * n_rope
    qi = _dot(cq, w_ref[:, o:o + n_idx])
    qis = _dot(cq, w_ref[:, o + n_idx:o + 2 * n_idx])
    rep = n_idx // ci_t.shape[2]
    qi = (qi * jnp.tile(ci_t[0], (1, rep)) + qis * jnp.tile(si_t[0], (1, rep))).astype(BF16)
    for h in range(IDX_HEADS):
        qidx_ref[0, :, h] = qi[:, h * IDX_DIM:(h + 1) * IDX_DIM].reshape(nb, Q_BLOCK, IDX_DIM)


def _dsa_q(cq, w_q, cq_t, sq_t, ci_t, si_t, tm=512):
    b, s, rq = cq.shape
    nb = tm // Q_BLOCK

    def tok(w):
        return pl.BlockSpec((1, tm, w), lambda bi, i: (bi, i, 0))

    return pl.pallas_call(
        _dsa_q_kernel,
        grid=(b, s // tm),
        in_specs=[tok(rq), pl.BlockSpec(w_q.shape, lambda bi, i: (0, 0)),
                  tok(128), tok(128), tok(128), tok(128)],
        out_specs=[pl.BlockSpec((1, nb, A_HEADS, Q_BLOCK, A_KEY), lambda bi, i: (bi, i, 0, 0, 0)),
                   pl.BlockSpec((1, nb, IDX_HEADS, Q_BLOCK, IDX_DIM), lambda bi, i: (bi, i, 0, 0, 0))],
        out_shape=[jax.ShapeDtypeStruct((b, s // Q_BLOCK, A_HEADS, Q_BLOCK, A_KEY), BF16),
                   jax.ShapeDtypeStruct((b, s // Q_BLOCK, IDX_HEADS, Q_BLOCK, IDX_DIM), BF16)],
        compiler_params=_params("parallel", "parallel"),
    )(cq, w_q, cq_t, sq_t, ci_t, si_t)


KEY_NEG_INF = -2139095041


def _key_to_float(key):
    return pltpu.bitcast(key ^ ((key >> 31) & jnp.int32(0x7FFFFFFF)), F32)


def _dsa_idx_kernel(qidx_ref, kidx_ref, widx_ref, bias_ref, score_scr, last_scr, *, topk, tk):
    qb = pl.program_id(1)
    s = kidx_ref.shape[1]
    n_tiles = ((qb + 1) * Q_BLOCK + tk - 1) // tk
    q_all = qidx_ref[0, 0].reshape(IDX_HEADS * Q_BLOCK, IDX_DIM)
    w = widx_ref[0]
    q_chunk = (qb * Q_BLOCK + lax.broadcasted_iota(jnp.int32, (tk, Q_BLOCK), 1)) // CHUNK
    key0 = lax.broadcasted_iota(jnp.int32, (tk, Q_BLOCK), 0)

    def tile_off(t):
        return pl.multiple_of(t * tk, tk)

    def allowed_tile(off):
        return ((key0 + off) // CHUNK) <= q_chunk

    def score_tile(t, carry):
        off = tile_off(t)
        rel = _dot(kidx_ref[0, pl.ds(off, tk), :], q_all, NT_DIMS)
        sc = jnp.zeros((tk, Q_BLOCK), F32)
        for h in range(IDX_HEADS):
            sc = sc + jnp.maximum(rel[:, h * Q_BLOCK:(h + 1) * Q_BLOCK], 0.0) * w[h:h + 1, :]
        score_scr[pl.ds(off, tk), :] = jnp.where(allowed_tile(off), sc, -jnp.inf)
        return carry

    lax.fori_loop(0, n_tiles, score_tile, 0)

    cnt_rows = 64

    def count(pred):
        def body(t, acc):
            off = tile_off(t)
            hit = jnp.where(pred(score_scr[pl.ds(off, tk), :], off), 1.0, 0.0)
            for j in range(tk // cnt_rows):
                acc = acc + hit[j * cnt_rows:(j + 1) * cnt_rows]
            return acc

        acc = lax.fori_loop(0, n_tiles, body, jnp.zeros((cnt_rows, Q_BLOCK), F32))
        return jnp.sum(acc, axis=0, keepdims=True)

    kf = jnp.float32(topk)
    n_visited = (n_tiles * tk).astype(F32)

    def count_key(cand):
        cand_f = _key_to_float(cand)
        return jnp.where(cand < KEY_NEG_INF, n_visited, count(lambda sc, off: sc >= cand_f))

    thr = jnp.where(count_key(jnp.zeros((1, Q_BLOCK), jnp.int32)) >= kf, jnp.int32(0), jnp.int32(-2 ** 31))

    def bit_step(i, thr):
        cand = thr | lax.shift_left(jnp.int32(1), jnp.int32(30) - i)
        return jnp.where(count_key(cand) >= kf, cand, thr)

    thr = lax.fori_loop(0, 31, bit_step, thr)
    thr_f = _key_to_float(thr)
    take_all = thr < KEY_NEG_INF

    last_scr[...] = jnp.full((1, Q_BLOCK), s - 1, jnp.int32)
    n_ge = jnp.where(take_all, kf, count(lambda sc, off: sc >= thr_f))

    @pl.when(jnp.max(n_ge) > kf)
    def _():
        need = kf - count(lambda sc, off: sc > thr_f)

        def idx_step(i, last):
            cand = last | lax.shift_left(jnp.int32(1), jnp.int32((s - 1).bit_length() - 1) - i)
            below = count(lambda sc, off: (sc == thr_f) & (key0 + off < cand))
            return jnp.where(below < need, cand, last)

        last = lax.fori_loop(0, (s - 1).bit_length(), idx_step, jnp.zeros((1, Q_BLOCK), jnp.int32))
        last_scr[...] = jnp.where(n_ge > kf, last, s - 1)

    last = last_scr[...]

    def write_tile(t, carry):
        off = tile_off(t)
        sc = score_scr[pl.ds(off, tk), :]
        sel = ((sc > thr_f) | ((sc == thr_f) & (key0 + off <= last)) | take_all) & allowed_tile(off)
        bias_ref[0, :, pl.ds(off, tk)] = jnp.where(sel, 0.0, MASK_VALUE).T.astype(BF16)
        return carry

    lax.fori_loop(0, n_tiles, write_tile, 0)

    def mask_tile(t, carry):
        bias_ref[0, :, pl.ds(tile_off(t), tk)] = jnp.full((Q_BLOCK, tk), MASK_VALUE, BF16)
        return carry

    lax.fori_loop(n_tiles, s // tk, mask_tile, 0)


def _dsa_idx(qidx, kidx, widx_t, topk, tk=512):
    b, nb = qidx.shape[:2]
    s = kidx.shape[1]
    tk = min(tk, s)
    kern = functools.partial(_dsa_idx_kernel, topk=topk, tk=tk)
    return pl.pallas_call(
        kern,
        grid=(b, nb),
        in_specs=[pl.BlockSpec((1, 1, IDX_HEADS, Q_BLOCK, IDX_DIM), lambda bi, i: (bi, i, 0, 0, 0)),
                  pl.BlockSpec((1, s, IDX_DIM), lambda bi, i: (bi, 0, 0)),
                  pl.BlockSpec((1, IDX_HEADS, Q_BLOCK), lambda bi, i: (bi, 0, i))],
        out_specs=pl.BlockSpec((1, Q_BLOCK, s), lambda bi, i: (bi, i, 0)),
        out_shape=jax.ShapeDtypeStruct((b, s, s), BF16),
        scratch_shapes=[pltpu.VMEM((s, Q_BLOCK), F32), pltpu.VMEM((1, Q_BLOCK), jnp.int32)],
        compiler_params=_params("parallel", "parallel"),
    )(qidx, kidx, widx_t)


def _dsa_attn_kernel(q_ref, keys_ref, bias_ref, o_ref, m_scr, l_scr, acc_scr, *, tk, n_strips):
    qb = pl.program_id(1)
    hs = A_HEADS // n_strips
    rows = hs * Q_BLOCK
    m_scr[...] = jnp.full_like(m_scr, -jnp.inf)
    l_scr[...] = jnp.zeros_like(l_scr)
    acc_scr[...] = jnp.zeros_like(acc_scr)
    n_tiles = ((qb + 1) * Q_BLOCK + tk - 1) // tk

    def body(t, carry):
        off = pl.multiple_of(t * tk, tk)
        kt = keys_ref[0, pl.ds(off, tk), :]
        kv = kt[:, :A_KV_RANK]
        bias = bias_ref[0, :, pl.ds(off, tk)].astype(F32)

        def scores(i):
            q = q_ref[0, 0, i * hs:(i + 1) * hs].reshape(rows, A_KEY)
            return _dot(q, kt, NT_DIMS)

        def softmax_update(i, sc):
            rs = slice(i * rows, (i + 1) * rows)
            sc = (sc.reshape(hs, Q_BLOCK, tk) + bias[None]).reshape(rows, tk)
            m_prev = m_scr[rs]
            m_new = jnp.maximum(m_prev, jnp.max(sc, axis=1, keepdims=True))
            alpha = jnp.exp(m_prev - m_new)
            p = jnp.exp(sc - jnp.tile(m_new, (1, tk // 128)))
            l_scr[rs] = alpha * l_scr[rs] + jnp.sum(p, axis=1, keepdims=True)
            m_scr[rs] = m_new
            return p.astype(BF16), alpha

        def values(i, p, alpha):
            rs = slice(i * rows, (i + 1) * rows)
            acc_scr[rs] = acc_scr[rs] * jnp.tile(alpha, (1, A_KV_RANK // 128)) + _dot(p, kv)

        sc = {0: scores(0)}
        pa = {}
        for i in range(n_strips):
            if i + 1 < n_strips:
                sc[i + 1] = scores(i + 1)
            pa[i] = softmax_update(i, sc.pop(i))
            if i > 0:
                values(i - 1, *pa.pop(i - 1))
        values(n_strips - 1, *pa.pop(n_strips - 1))
        return carry

    lax.fori_loop(0, n_tiles, body, 0)
    o = acc_scr[...] / jnp.tile(l_scr[...], (1, A_KV_RANK // 128))
    o_ref[0, 0] = o.astype(BF16).reshape(A_HEADS, Q_BLOCK, A_KV_RANK)


def _dsa_attn(qatt, keys, bias, tk=512, n_strips=4):
    b, nb = qatt.shape[:2]
    s = keys.shape[1]
    tk = min(tk, s)
    rows = A_HEADS * Q_BLOCK
    kern = functools.partial(_dsa_attn_kernel, tk=tk, n_strips=n_strips)
    return pl.pallas_call(
        kern,
        grid=(b, nb),
        in_specs=[pl.BlockSpec((1, 1, A_HEADS, Q_BLOCK, A_KEY), lambda bi, i: (bi, i, 0, 0, 0)),
                  pl.BlockSpec((1, s, A_KEY), lambda bi, i: (bi, 0, 0)),
                  pl.BlockSpec((1, Q_BLOCK, s), lambda bi, i: (bi, i, 0))],
        out_specs=pl.BlockSpec((1, 1, A_HEADS, Q_BLOCK, A_KV_RANK), lambda bi, i: (bi, i, 0, 0, 0)),
        out_shape=jax.ShapeDtypeStruct((b, nb, A_HEADS, Q_BLOCK, A_KV_RANK), BF16),
        scratch_shapes=[pltpu.VMEM((rows, 128), F32), pltpu.VMEM((rows, 128), F32),
                        pltpu.VMEM((rows, A_KV_RANK), F32)],
        compiler_params=_params("parallel", "parallel"),
    )(qatt, keys, bias)


def _dsa_out_kernel(ol_ref, wuv_ref, wo_ref, x_ref, g_ref, o_ref, y_scr):
    nb = ol_ref.shape[1]
    tm = nb * Q_BLOCK
    for h in range(A_HEADS):
        oh = ol_ref[0, :, h].reshape(tm, A_KV_RANK)
        y_scr[:, h * A_V:(h + 1) * A_V] = _dot(oh, wuv_ref[h]).astype(BF16)
    o_ref[0] = x_ref[0] + g_ref[0] * _dot(y_scr[...], wo_ref[...])


def _dsa_out(olat, w_uv_h, w_o, x, g, tm=512):
    b, s, d = x.shape
    nb = tm // Q_BLOCK
    return pl.pallas_call(
        _dsa_out_kernel,
        grid=(b, s // tm),
        in_specs=[pl.BlockSpec((1, nb, A_HEADS, Q_BLOCK, A_KV_RANK), lambda bi, i: (bi, i, 0, 0, 0)),
                  pl.BlockSpec(w_uv_h.shape, lambda bi, i: (0, 0, 0)),
                  pl.BlockSpec(w_o.shape, lambda bi, i: (0, 0)),
                  pl.BlockSpec((1, tm, d), lambda bi, i: (bi, i, 0)),
                  pl.BlockSpec((1, 1, d), lambda bi, i: (bi, 0, 0))],
        out_specs=pl.BlockSpec((1, tm, d), lambda bi, i: (bi, i, 0)),
        out_shape=jax.ShapeDtypeStruct((b, s, d), F32),
        scratch_shapes=[pltpu.VMEM((tm, A_HEADS * A_V), BF16)],
        compiler_params=_params("parallel", "parallel"),
    )(olat, w_uv_h, w_o, x, g)


def _rope_tables(positions):
    def angles(dim):
        inv = 1.0 / (ROPE_BASE ** (jnp.arange(0, dim, 2, dtype=F32) / dim))
        ang = positions.astype(F32)[..., None] * inv
        return jnp.cos(ang), jnp.sin(ang)

    cos_a, sin_a = angles(A_ROPE)
    cos_i, sin_i = angles(IDX_ROPE)
    return cos_a, sin_a, cos_i, sin_i


def _dsa_mixer(x, sc, sh, g, positions, w_in, qg, kvg, w_uq, w_qidx, lng, lnb, w_uk, w_uv, w_o):
    b, s, d = x.shape
    topk = min(TOPK_MAX, s // 4)
    cos_a, sin_a, cos_i, sin_i = _rope_tables(positions)
    w_in_p = jnp.pad(w_in, ((0, 0), (0, A_IN_PAD - A_IN))).astype(BF16)
    cq, keys, kidx, widx = _dsa_in(x, sc, sh, w_in_p, qg.reshape(1, -1), kvg.reshape(1, -1),
                                   lng.reshape(1, -1), lnb.reshape(1, -1), cos_a, sin_a, cos_i, sin_i)

    rq = w_uq.shape[0]
    w_uq_h = w_uq.reshape(rq, A_HEADS, A_NOPE + A_ROPE)
    w_lat = _fold_uq_uk(jnp.transpose(w_uq_h[:, :, :A_NOPE], (1, 0, 2)), jnp.transpose(w_uk, (1, 2, 0)))
    w_lat = jnp.transpose(w_lat, (1, 0, 2)).reshape(rq, A_HEADS * A_KV_RANK)
    w_r = w_uq_h[:, :, A_NOPE:]
    w_rs = jnp.concatenate([w_r[..., A_ROPE // 2:], w_r[..., :A_ROPE // 2]], axis=-1)
    w_i = w_qidx.reshape(rq, IDX_HEADS, IDX_DIM)
    hr = IDX_ROPE // 2
    w_is = jnp.concatenate([w_i[..., hr:IDX_ROPE], w_i[..., :hr], jnp.zeros_like(w_i[..., IDX_ROPE:])], axis=-1)
    w_q = jnp.concatenate([w_lat, w_r.reshape(rq, -1).astype(BF16), w_rs.reshape(rq, -1).astype(BF16),
                           w_i.reshape(rq, -1).astype(BF16), w_is.reshape(rq, -1).astype(BF16)], axis=1)
    cq_t = jnp.concatenate([cos_a, cos_a, cos_a, cos_a], axis=-1)
    sq_t = jnp.concatenate([-sin_a, sin_a, -sin_a, sin_a], axis=-1)
    one = jnp.ones(cos_i.shape[:-1] + (IDX_DIM - IDX_ROPE,), F32)
    ci_t = jnp.concatenate([cos_i, cos_i, one, cos_i, cos_i, one], axis=-1)
    si_t = jnp.concatenate([-sin_i, sin_i, 0 * one, -sin_i, sin_i, 0 * one], axis=-1)
    qatt, qidx = _dsa_q(cq, w_q, cq_t, sq_t, ci_t, si_t)

    bias = _dsa_idx(qidx, kidx, jnp.transpose(widx, (0, 2, 1)), topk)
    olat = _dsa_attn(qatt, keys, bias)
    w_uv_h = jnp.transpose(w_uv, (1, 0, 2)).astype(BF16)
    return _dsa_out(olat, w_uv_h, w_o.astype(BF16), x, g)


def _rwkv_in_kernel(x_ref, xp_ref, sc_ref, sh_ref, mu_ref, wl1_ref, wr_ref, wk_ref, wv_ref,
                    w2_ref, a2_ref, g2_ref, vec_ref, seg_ref,
                    r_out, lw_out, k_out, v_out, kk_out, a_out, g_out, bonus_out,
                    mix_scr, lora_scr):
    i = pl.program_id(1)
    j = pl.program_id(2)

    @pl.when(j == 0)
    def _():
        sc = sc_ref[0]
        sh = sh_ref[0]
        h = _rms_mod(x_ref[0], sc, sh)
        hp = _rms_mod(xp_ref[0, 7:8, :], sc, sh)
        hp = jnp.where(i == 0, 0.0, hp)
        rows = lax.broadcasted_iota(jnp.int32, h.shape, 0)
        h_prev = jnp.where(rows == 0, hp, pltpu.roll(h, 1, axis=0))
        delta = h_prev - h
        for m in range(6):
            mix_scr[m] = (h + delta * mu_ref[m:m + 1, :]).astype(BF16)
        tw = jnp.tanh(_dot(mix_scr[1], wl1_ref[:, :LORA_PAD]))
        ta = _dot(mix_scr[4], wl1_ref[:, LORA_PAD:2 * LORA_PAD])
        tg = _sigmoid(_dot(mix_scr[5], wl1_ref[:, 2 * LORA_PAD:]))
        lora_scr[:, :LORA_PAD] = tw.astype(BF16)
        lora_scr[:, LORA_PAD:2 * LORA_PAD] = ta.astype(BF16)
        lora_scr[:, 2 * LORA_PAD:] = tg.astype(BF16)

    w0 = vec_ref[0:1, :]
    a0 = vec_ref[1:2, :]
    k_k = vec_ref[2:3, :]
    k_a = vec_ref[3:4, :]
    r_k = vec_ref[4:5, :]
    seg = seg_ref[...]
    r = _dot(mix_scr[0], wr_ref[...])
    k = _dot(mix_scr[2], wk_ref[...])
    v = _dot(mix_scr[3], wv_ref[...])
    z = -(w0 + _dot(lora_scr[:, :LORA_PAD], w2_ref[...]))
    softplus = jnp.maximum(z, 0.0) + jnp.log(1.0 + jnp.exp(-jnp.abs(z)))
    lw = -jnp.exp(-softplus - 0.5)
    a = _sigmoid(a0 + _dot(lora_scr[:, LORA_PAD:2 * LORA_PAD], a2_ref[...]))
    g = _dot(lora_scr[:, 2 * LORA_PAD:], g2_ref[...])
    kk = k * k_k
    nrm = jnp.sqrt(_dot2_exact_rhs(kk * kk, seg))
    kk = kk / jnp.maximum(nrm, 1e-12)
    k2 = k * (1.0 + (a - 1.0) * k_a)
    bonus = _dot2_exact_rhs(r * k2 * r_k, seg) * v
    r_out[0] = r.astype(BF16)
    lw_out[0] = lw
    k_out[0] = k2.astype(BF16)
    v_out[0] = v.astype(BF16)
    kk_out[0] = kk.astype(BF16)
    a_out[0] = a.astype(BF16)
    g_out[0] = g.astype(BF16)
    bonus_out[0] = bonus.astype(BF16)


def _rwkv_in(x, sc, sh, mu, wl1, w_r, w_k, w_v, w2, a2, g2, vec, seg, tm=512, tn=256):
    b, s, d = x.shape
    sub = 8
    vecs = pl.BlockSpec((1, 1, d), lambda bi, i, j: (bi, 0, 0))
    col = lambda rows: pl.BlockSpec((rows, tn), lambda bi, i, j: (0, j))
    out = pl.BlockSpec((1, tm, tn), lambda bi, i, j: (bi, i, j))
    return pl.pallas_call(
        _rwkv_in_kernel,
        grid=(b, s // tm, d // tn),
        in_specs=[pl.BlockSpec((1, tm, d), lambda bi, i, j: (bi, i, 0)),
                  pl.BlockSpec((1, sub, d), lambda bi, i, j: (bi, jnp.maximum(i * (tm // sub) - 1, 0), 0)),
                  vecs, vecs,
                  pl.BlockSpec(mu.shape, lambda bi, i, j: (0, 0)),
                  pl.BlockSpec(wl1.shape, lambda bi, i, j: (0, 0)),
                  col(d), col(d), col(d), col(LORA_PAD), col(LORA_PAD), col(R_GATE_LORA),
                  col(8), pl.BlockSpec(seg.shape, lambda bi, i, j: (0, 0))],
        out_specs=[out] * 8,
        out_shape=[jax.ShapeDtypeStruct((b, s, d), F32 if i == 1 else BF16) for i in range(8)],
        scratch_shapes=[pltpu.VMEM((6, tm, d), BF16), pltpu.VMEM((tm, 2 * LORA_PAD + R_GATE_LORA), BF16)],
        compiler_params=_params("parallel", "parallel", "arbitrary"),
    )(x, x, sc, sh, mu, wl1, w_r, w_k, w_v, w2, a2, g2, vec, seg)


def _rwkv_rec_kernel(r_ref, lw_ref, k_ref, v_ref, kk_ref, a_ref, o_ref, s_scr, *, n_chunks, n_heads):
    t = pl.program_id(2)

    @pl.when(t == 0)
    def _():
        s_scr[...] = jnp.zeros_like(s_scr)

    c_len = CHUNK
    n = R_HEAD
    ri = lax.broadcasted_iota(jnp.int32, (c_len, c_len), 0)
    ci = lax.broadcasted_iota(jnp.int32, (c_len, c_len), 1)
    strict = ri > ci
    incl = ri >= ci
    tri = jnp.where(incl, 1.0, 0.0).astype(BF16)
    items = [(c, h) for c in range(n_chunks) for h in range(n_heads)]

    pre = []
    for c in range(n_chunks):
        sl = slice(c * c_len, (c + 1) * c_len)
        lw = lw_ref[0, sl, :]
        k = k_ref[0, sl, :].astype(F32)
        kk = kk_ref[0, sl, :].astype(F32)
        kb = kk * a_ref[0, sl, :].astype(F32)
        cum = _cumsum_rows(tri, lw)
        cum_c = cum[c_len - 1:c_len, :]
        p_inv = jnp.exp(-cum)
        p_hat = jnp.exp(cum_c - cum)
        pre.append(dict(
            r_t=r_ref[0, sl, :].astype(F32) * jnp.exp(cum), a_t=-kk * jnp.exp(cum - lw), b_t=kb * p_inv,
            k_t=k * p_inv, b_h=kb * p_hat, k_h=k * p_hat, p_c=jnp.exp(cum_c), v=v_ref[0, sl, :].astype(F32)))

    def head(c, h, name):
        return pre[c][name][:, h * n:(h + 1) * n]

    amat = {}
    for c, h in items:
        ar = jnp.concatenate([head(c, h, "a_t"), head(c, h, "r_t")], axis=0)
        bk = jnp.concatenate([head(c, h, "b_t"), head(c, h, "k_t")], axis=0)
        amat[c, h] = _mm(ar, bk, NT_DIMS, PASSES_PAIR)
    l_ab, m_rb, lv = {}, {}, {}
    for c, h in items:
        am = amat[c, h]
        l_ab[c, h] = jnp.where(strict, am[:c_len, :c_len], 0.0)
        m_rb[c, h] = jnp.where(incl, am[c_len:, :c_len], 0.0)
        lm = jnp.concatenate([jnp.where(strict, am[:c_len, c_len:], 0.0),
                              jnp.where(incl, am[c_len:, c_len:], 0.0)], axis=0)
        lv[c, h] = _mm(lm, head(c, h, "v"), NN_DIMS, PASSES_PAIR)
    z = {}
    for c, h in items:
        z[c, h] = jnp.concatenate([head(c, h, "a_t"), lv[c, h][:c_len], l_ab[c, h]], axis=1)
    for it in range(6):
        for c, h in items:
            zz = z[c, h]
            pw = zz[:, 2 * n:]
            if it < 5:
                x = _mm(pw, zz, NN_DIMS, PASSES_SOLVE)
                z[c, h] = jnp.concatenate([zz[:, :2 * n] + x[:, :2 * n], x[:, 2 * n:]], axis=1)
            else:
                z[c, h] = zz[:, :2 * n] + _mm(pw, zz[:, :2 * n], NN_DIMS, PASSES_SOLVE)
    state = [s_scr[h] for h in range(n_heads)]
    for c in range(n_chunks):
        u = [_mm(z[c, h][:, :n], state[h], NT_DIMS, PASSES_STATE) + z[c, h][:, n:] for h in range(n_heads)]
        o = [_mm(head(c, h, "r_t"), state[h], NT_DIMS, PASSES_STATE) + _mm(m_rb[c, h], u[h], NN_DIMS, PASSES_STATE)
             + lv[c, h][c_len:] for h in range(n_heads)]
        state = [state[h] * head(c, h, "p_c")
                 + _mm(u[h], head(c, h, "b_h"), TN_DIMS, PASSES_STATE)
                 + _mm(head(c, h, "v"), head(c, h, "k_h"), TN_DIMS, PASSES_STATE) for h in range(n_heads)]
        o_ref[0, c * c_len:(c + 1) * c_len, :] = jnp.concatenate(o, axis=1)
    for h in range(n_heads):
        s_scr[h] = state[h]


def _mm(a, b, dims=NN_DIMS, passes=1):
    if passes == 1:
        return _dot(a.astype(BF16), b.astype(BF16), dims)
    return _dot3(a, b, dims)


def _cumsum_rows(tri, x):
    hi, lo = _split(x)
    lo2 = (x - hi.astype(F32) - lo.astype(F32)).astype(BF16)
    return _dot(tri, hi) + _dot(tri, lo) + _dot(tri, lo2)


def _rwkv_rec(r, lw, k, v, kk, a, tb=256, lanes=512):
    b, s, d = r.shape
    tb = min(tb, s)
    n_heads = lanes // R_HEAD
    kern = functools.partial(_rwkv_rec_kernel, n_chunks=tb // CHUNK, n_heads=n_heads)
    blk = pl.BlockSpec((1, tb, lanes), lambda bi, hi, ti: (bi, ti, hi))
    return pl.pallas_call(
        kern,
        grid=(b, d // lanes, s // tb),
        in_specs=[blk] * 6,
        out_specs=blk,
        out_shape=jax.ShapeDtypeStruct((b, s, d), F32),
        scratch_shapes=[pltpu.VMEM((n_heads, R_HEAD, R_HEAD), F32)],
        compiler_params=_params("parallel", "parallel", "arbitrary"),
    )(r, lw, k, v, kk, a)


def _rwkv_out_kernel(o_ref, g_ref, bonus_ref, x_ref, gate_ref, gng_ref, gnb_ref, seg_ref, wo_ref,
                     out_ref, y_scr):
    d = x_ref.shape[2]
    tn = seg_ref.shape[0]
    seg = seg_ref[...]
    inv_n = 1.0 / R_HEAD
    for jt in range(d // tn):
        sl = slice(jt * tn, (jt + 1) * tn)
        o = o_ref[0, :, sl]
        mu = _dot2_exact_rhs(o, seg) * inv_n
        dv = o - mu
        var = _dot2_exact_rhs(dv * dv, seg) * inv_n
        ln = dv * lax.rsqrt(var + R_GN_EPS)
        y = ((ln * gng_ref[:, sl] + gnb_ref[:, sl] + bonus_ref[0, :, sl].astype(F32))
             * g_ref[0, :, sl].astype(F32))
        y_scr[:, sl] = y.astype(BF16)
    out_ref[0] = x_ref[0] + gate_ref[0] * _dot(y_scr[...], wo_ref[...])


def _rwkv_out(o, g, bonus, x, gate, gng, gnb, seg, w_o, tm=256):
    b, s, d = x.shape
    tok = pl.BlockSpec((1, tm, d), lambda bi, i: (bi, i, 0))
    full = lambda a: pl.BlockSpec(a.shape, lambda bi, i: (0,) * a.ndim)
    return pl.pallas_call(
        _rwkv_out_kernel,
        grid=(b, s // tm),
        in_specs=[tok, tok, tok, tok, pl.BlockSpec((1, 1, d), lambda bi, i: (bi, 0, 0)),
                  full(gng), full(gnb), full(seg), full(w_o)],
        out_specs=tok,
        out_shape=jax.ShapeDtypeStruct((b, s, d), F32),
        scratch_shapes=[pltpu.VMEM((tm, d), BF16)],
        compiler_params=_params("parallel", "parallel"),
    )(o, g, bonus, x, gate, gng, gnb, seg, w_o)


def _rwkv_mixer(x, sc, sh, gate, mu, w_r, w_k, w_v, w_o, w0, w_w1, w_w2, a0, w_a1, w_a2, w_g1, w_g2,
                k_k, k_a, r_k, gn_g, gn_b):
    d = x.shape[2]
    tn = 256
    lora = w_w1.shape[1]
    padc = lambda w: jnp.pad(w, ((0, 0), (0, LORA_PAD - lora)))
    padr = lambda w: jnp.pad(w, ((0, LORA_PAD - lora), (0, 0)))
    wl1 = jnp.concatenate([padc(w_w1), padc(w_a1), w_g1], axis=1).astype(BF16)
    vec = jnp.stack([w0, a0, k_k, k_a, r_k.reshape(-1), w0 * 0, w0 * 0, w0 * 0])
    idx = jnp.arange(tn) // R_HEAD
    seg = (idx[:, None] == idx[None, :]).astype(BF16)
    r, lw, k2, v, kk, a, g, bonus = _rwkv_in(
        x, sc, sh, mu, wl1, w_r.astype(BF16), w_k.astype(BF16), w_v.astype(BF16),
        padr(w_w2).astype(BF16), padr(w_a2).astype(BF16), w_g2.astype(BF16), vec, seg, tn=tn)
    o = _rwkv_rec(r, lw, k2, v, kk, a)
    return _rwkv_out(o, g, bonus, x, gate, gn_g.reshape(1, d), gn_b.reshape(1, d), seg, w_o.astype(BF16))


def kernel(x, c, positions, ada_w, ada_b, mlp_w1, mlp_w2, final_g, a_w_in, a_q_norm_g, a_kv_norm_g, a_w_uq,
           a_w_qidx, a_kidx_ln_g, a_kidx_ln_b, a_w_uk, a_w_uv, a_w_o, b_mu, b_w_r, b_w_k, b_w_v, b_w_o, b_w0,
           b_w_w1, b_w_w2, b_a0, b_w_a1, b_w_a2, b_w_g1, b_w_g2, b_k_k, b_k_a, b_r_k, b_gn_g, b_gn_b):
    depth = ada_w.shape[0]
    d = x.shape[2]
    mod = _ada_mod(c, ada_w, ada_b)
    for i in range(depth):
        sh1, sc1, g1, sh2, sc2, g2 = [mod[i, :, None, m * d:(m + 1) * d] for m in range(6)]
        j = i // 2
        if i % 2 == 0:
            x = _dsa_mixer(x, sc1, sh1, g1, positions, a_w_in[j], a_q_norm_g[j], a_kv_norm_g[j], a_w_uq[j],
                           a_w_qidx[j], a_kidx_ln_g[j], a_kidx_ln_b[j], a_w_uk[j], a_w_uv[j], a_w_o[j])
        else:
            x = _rwkv_mixer(x, sc1, sh1, g1, b_mu[j], b_w_r[j], b_w_k[j], b_w_v[j], b_w_o[j], b_w0[j],
                            b_w_w1[j], b_w_w2[j], b_a0[j], b_w_a1[j], b_w_a2[j], b_w_g1[j], b_w_g2[j],
                            b_k_k[j], b_k_a[j], b_r_k[j], b_gn_g[j], b_gn_b[j])
        x = _mlp(x, sc2, sh2, g2, mlp_w1[i].astype(BF16), mlp_w2[i].astype(BF16), final_g,
                 final_norm=(i == depth - 1))
    return x
```

```python
import functools

import jax
import jax.numpy as jnp
from jax import lax
from jax.experimental import pallas as pl
from jax.experimental.pallas import tpu as pltpu

F32 = jnp.float32
BF16 = jnp.bfloat16

D_MODEL = 2048
DEPTH = 2
CHUNK = 64
EPS = 1e-6
A_HEADS = 16
A_NOPE = 128
A_ROPE = 64
A_V = 128
A_Q_RANK = 512
A_KV_RANK = 256
A_KEY = A_KV_RANK + A_ROPE
IDX_HEADS = 16
IDX_DIM = 64
IDX_ROPE = 32
TOPK_MAX = 256
Q_BLOCK = 128
ROPE_BASE = 10000.0
A_SCALE = (A_NOPE + A_ROPE) ** -0.5
A_IN = A_Q_RANK + A_KV_RANK + A_ROPE + IDX_DIM + IDX_HEADS
A_IN_PAD = 1024
R_HEAD = 64
R_GN_EPS = R_HEAD * 1e-5
LORA_PAD = 128
R_GATE_LORA = 256

VMEM_LIMIT_BYTES = 56 * 1024 * 1024
MASK_VALUE = -1e30

NT_DIMS = (((1,), (1,)), ((), ()))
TN_DIMS = (((0,), (0,)), ((), ()))
NN_DIMS = (((1,), (0,)), ((), ()))

PASSES_PAIR = 1
PASSES_SOLVE = 1
PASSES_STATE = 1


def _params(*sem):
    return pltpu.CompilerParams(dimension_semantics=sem, vmem_limit_bytes=VMEM_LIMIT_BYTES)


def _dot(a, b, dims=NN_DIMS):
    return lax.dot_general(a, b, dims, preferred_element_type=F32)


def _split(x):
    hi = x.astype(BF16)
    lo = (x - hi.astype(F32)).astype(BF16)
    return hi, lo


def _dot3(a, b, dims=NN_DIMS):
    ah, al = _split(a)
    bh, bl = _split(b)
    return _dot(ah, bh, dims) + _dot(ah, bl, dims) + _dot(al, bh, dims)


def _segsum(x, seg):
    return _dot(x.astype(BF16), seg)


def _rms(x):
    return x * lax.rsqrt(jnp.mean(x * x, axis=-1, keepdims=True) + EPS)


def _rms_mod(x, sc, sh):
    return _rms(x) * (1.0 + sc) + sh


def _sigmoid(x):
    return 1.0 / (1.0 + jnp.exp(-x))


def _rope_half(x, c, s):
    d = x.shape[-1] // 2
    x1 = x[:, :d]
    x2 = x[:, d:]
    return jnp.concatenate([x1 * c - x2 * s, x2 * c + x1 * s], axis=-1)


def _mod_kernel(c_ref, w_ref, b_ref, o_ref):
    c = c_ref[...]
    a = (c * _sigmoid(c)).astype(BF16)
    o_ref[0] = _dot(a, w_ref[0].astype(BF16)) + b_ref[0]


def _ada_mod(c, ada_w, ada_b):
    depth, d, n = ada_w.shape
    b = c.shape[0]
    tn = 1024
    return pl.pallas_call(
        _mod_kernel,
        grid=(depth, n // tn),
        in_specs=[
            pl.BlockSpec((b, d), lambda l, j: (0, 0)),
            pl.BlockSpec((1, d, tn), lambda l, j: (l, 0, j)),
            pl.BlockSpec((1, 1, tn), lambda l, j: (l, 0, j)),
        ],
        out_specs=pl.BlockSpec((1, b, tn), lambda l, j: (l, 0, j)),
        out_shape=jax.ShapeDtypeStruct((depth, b, n), F32),
        compiler_params=_params("arbitrary", "arbitrary"),
    )(c, ada_w, ada_b.reshape(depth, 1, n))


def _mlp_kernel(x_ref, xn_ref, sc_ref, sh_ref, scn_ref, shn_ref, g_ref, w1_ref, w2_ref, fg_ref, o_ref,
                h_scr, acc_scr, *, final_norm):
    f = pl.program_id(2)
    last = pl.num_programs(2) - 1

    @pl.when((pl.program_id(0) == 0) & (pl.program_id(1) == 0) & (f == 0))
    def _():
        h_scr[...] = _rms_mod(x_ref[0], sc_ref[0], sh_ref[0]).astype(BF16)
        acc_scr[...] = jnp.zeros_like(acc_scr)

    def hidden():
        a = jnp.maximum(_dot(h_scr[...], w1_ref[...]), 0.0)
        return _dot((a * a).astype(BF16), w2_ref[...])

    @pl.when(f < last)
    def _():
        acc_scr[...] += hidden()

    @pl.when(f == last)
    def _():
        y = x_ref[0] + g_ref[0] * (acc_scr[...] + hidden())
        if final_norm:
            y = _rms(y) * fg_ref[...]
        o_ref[0] = y
        h_scr[...] = _rms_mod(xn_ref[0], scn_ref[0], shn_ref[0]).astype(BF16)
        acc_scr[...] = jnp.zeros_like(acc_scr)


def _mlp(x, sc, sh, g, w1, w2, final_g, final_norm, tm=512, tf=1024):
    b, s, d = x.shape
    dff = w1.shape[1]
    n_i = s // tm
    kern = functools.partial(_mlp_kernel, final_norm=final_norm)

    def next_block(bi, i):
        return jnp.minimum(bi * n_i + i + 1, b * n_i - 1)

    vec = pl.BlockSpec((1, 1, d), lambda bi, i, f: (bi, 0, 0))
    vec_next = pl.BlockSpec((1, 1, d), lambda bi, i, f: (next_block(bi, i) // n_i, 0, 0))
    return pl.pallas_call(
        kern,
        grid=(b, n_i, dff // tf),
        in_specs=[
            pl.BlockSpec((1, tm, d), lambda bi, i, f: (bi, i, 0)),
            pl.BlockSpec((1, tm, d), lambda bi, i, f: (next_block(bi, i) // n_i, next_block(bi, i) % n_i, 0)),
            vec, vec, vec_next, vec_next, vec,
            pl.BlockSpec((d, tf), lambda bi, i, f: (0, f)),
            pl.BlockSpec((tf, d), lambda bi, i, f: (f, 0)),
            pl.BlockSpec((1, d), lambda bi, i, f: (0, 0)),
        ],
        out_specs=pl.BlockSpec((1, tm, d), lambda bi, i, f: (bi, i, 0)),
        out_shape=jax.ShapeDtypeStruct((b, s, d), F32),
        scratch_shapes=[pltpu.VMEM((tm, d), BF16), pltpu.VMEM((tm, d), F32)],
        compiler_params=_params("arbitrary", "arbitrary", "arbitrary"),
    )(x, x, sc, sh, sc, sh, g, w1, w2, final_g.reshape(1, d))


def _dsa_in_kernel(x_ref, sc_ref, sh_ref, w_ref, qg_ref, kvg_ref, lng_ref, lnb_ref,
                   ca_ref, sa_ref, ci_ref, si_ref, cq_ref, keys_ref, kidx_ref, widx_ref):
    o_kv = A_Q_RANK
    o_kr = o_kv + A_KV_RANK
    o_ki = o_kr + A_ROPE
    o_wi = o_ki + IDX_DIM
    tm = x_ref.shape[1]
    halves = [slice(0, tm // 2), slice(tm // 2, tm)]
    hs = [_rms_mod(x_ref[0, rs], sc_ref[0], sh_ref[0]).astype(BF16) for rs in halves]
    projs = [_dot(h, w_ref[...]) for h in hs]
    for rs, proj in zip(halves, projs):
        cq_ref[0, rs] = (_rms(proj[:, :o_kv]) * qg_ref[...]).astype(BF16)
        keys_ref[0, rs, :A_KV_RANK] = (_rms(proj[:, o_kv:o_kr]) * kvg_ref[...]).astype(BF16)
        keys_ref[0, rs, A_KV_RANK:] = _rope_half(proj[:, o_kr:o_ki], ca_ref[0, rs], sa_ref[0, rs]).astype(BF16)
        ki = proj[:, o_ki:o_wi]
        mu = jnp.mean(ki, axis=-1, keepdims=True)
        kc = ki - mu
        var = jnp.mean(kc * kc, axis=-1, keepdims=True)
        ki = kc * lax.rsqrt(var + EPS) * lng_ref[...] + lnb_ref[...]
        ki = jnp.concatenate([_rope_half(ki[:, :IDX_ROPE], ci_ref[0, rs], si_ref[0, rs]), ki[:, IDX_ROPE:]],
                             axis=-1)
        kidx_ref[0, rs] = ki.astype(BF16)
        widx_ref[0, rs] = proj[:, o_wi:o_wi + IDX_HEADS] * (IDX_HEADS ** -0.5 * IDX_DIM ** -0.5)


def _dsa_in(x, sc, sh, w_in_p, qg, kvg, lng, lnb, cos_a, sin_a, cos_i, sin_i, tm=512):
    b, s, d = x.shape
    vec = pl.BlockSpec((1, 1, d), lambda bi, i: (bi, 0, 0))

    def full(a):
        return pl.BlockSpec(a.shape, lambda bi, i: (0,) * a.ndim)

    def tok(w):
        return pl.BlockSpec((1, tm, w), lambda bi, i: (bi, i, 0))

    return pl.pallas_call(
        _dsa_in_kernel,
        grid=(b, s // tm),
        in_specs=[tok(d), vec, vec, full(w_in_p), full(qg), full(kvg), full(lng), full(lnb),
                  tok(A_ROPE // 2), tok(A_ROPE // 2), tok(IDX_ROPE // 2), tok(IDX_ROPE // 2)],
        out_specs=[tok(A_Q_RANK), tok(A_KEY), tok(IDX_DIM), tok(IDX_HEADS)],
        out_shape=[jax.ShapeDtypeStruct((b, s, A_Q_RANK), BF16),
                   jax.ShapeDtypeStruct((b, s, A_KEY), BF16),
                   jax.ShapeDtypeStruct((b, s, IDX_DIM), BF16),
                   jax.ShapeDtypeStruct((b, s, IDX_HEADS), F32)],
        compiler_params=_params("parallel", "parallel"),
    )(x, sc, sh, w_in_p, qg, kvg, lng, lnb, cos_a, sin_a, cos_i, sin_i)


def _fold_kernel(a_ref, b_ref, o_ref):
    o_ref[0] = _dot3(a_ref[0], b_ref[0]).astype(BF16)


def _fold_uq_uk(w_uq_nope, w_uk_t):
    h, rq, dn = w_uq_nope.shape
    rkv = w_uk_t.shape[2]
    return pl.pallas_call(
        _fold_kernel,
        grid=(h,),
        in_specs=[pl.BlockSpec((1, rq, dn), lambda i: (i, 0, 0)),
                  pl.BlockSpec((1, dn, rkv), lambda i: (i, 0, 0))],
        out_specs=pl.BlockSpec((1, rq, rkv), lambda i: (i, 0, 0)),
        out_shape=jax.ShapeDtypeStruct((h, rq, rkv), BF16),
        compiler_params=_params("arbitrary"),
    )(w_uq_nope, w_uk_t)


def _dsa_q_kernel(cq_ref, w_ref, ca_ref, sa_ref, ci_ref, si_ref, qatt_ref, qidx_ref):
    cq = cq_ref[0]
    tm = cq.shape[0]
    ca, sa, ci, si = ca_ref[0], sa_ref[0], ci_ref[0], si_ref[0]
    one = jnp.ones((tm, IDX_DIM - IDX_ROPE), F32)
    cq_t = jnp.concatenate([ca, ca, ca, ca], axis=-1)
    sq_t = jnp.concatenate([-sa, sa, -sa, sa], axis=-1)
    ci_t = jnp.concatenate([ci, ci, one, ci, ci, one], axis=-1)
    si_t = jnp.concatenate([-si, si, 0.0 * one, -si, si, 0.0 * one], axis=-1)
    nb = tm // Q_BLOCK
    n_lat = A_HEADS * A_KV_RANK
    n_rope = A_HEADS * A_ROPE
    n_idx = IDX_HEADS * IDX_DIM
    for h in range(A_HEADS):
        lat = _dot(cq, w_ref[:, h * A_KV_RANK:(h + 1) * A_KV_RANK]) * A_SCALE
        qatt_ref[0, :, h, :, :A_KV_RANK] = lat.astype(BF16).reshape(nb, Q_BLOCK, A_KV_RANK)
    o = n_lat
    r = _dot(cq, w_ref[:, o:o + n_rope])
    rs = _dot(cq, w_ref[:, o + n_rope:o + 2 * n_rope])
    rep = n_rope // cq_t.shape[1]
    qr = ((r * jnp.tile(cq_t, (1, rep)) + rs * jnp.tile(sq_t, (1, rep))) * A_SCALE).astype(BF16)
    for h in range(A_HEADS):
        qatt_ref[0, :, h, :, A_KV_RANK:] = qr[:, h * A_ROPE:(h + 1) * A_ROPE].reshape(nb, Q_BLOCK, A_ROPE)
    o = n_lat + 2 * n_rope
    qi = _dot(cq, w_ref[:, o:o + n_idx])
    qis = _dot(cq, w_ref[:, o + n_idx:o + 2 * n_idx])
    rep = n_idx // ci_t.shape[1]
    qi = (qi * jnp.tile(ci_t, (1, rep)) + qis * jnp.tile(si_t, (1, rep))).astype(BF16)
    for h in range(IDX_HEADS):
        qidx_ref[0, :, h] = qi[:, h * IDX_DIM:(h + 1) * IDX_DIM].reshape(nb, Q_BLOCK, IDX_DIM)


def _dsa_q(cq, w_q, cos_a, sin_a, cos_i, sin_i, tm=512):
    b, s, rq = cq.shape
    nb = tm // Q_BLOCK

    def tok(w):
        return pl.BlockSpec((1, tm, w), lambda bi, i: (bi, i, 0))

    return pl.pallas_call(
        _dsa_q_kernel,
        grid=(b, s // tm),
        in_specs=[tok(rq), pl.BlockSpec(w_q.shape, lambda bi, i: (0, 0)),
                  tok(A_ROPE // 2), tok(A_ROPE // 2), tok(IDX_ROPE // 2), tok(IDX_ROPE // 2)],
        out_specs=[pl.BlockSpec((1, nb, A_HEADS, Q_BLOCK, A_KEY), lambda bi, i: (bi, i, 0, 0, 0)),
                   pl.BlockSpec((1, nb, IDX_HEADS, Q_BLOCK, IDX_DIM), lambda bi, i: (bi, i, 0, 0, 0))],
        out_shape=[jax.ShapeDtypeStruct((b, s // Q_BLOCK, A_HEADS, Q_BLOCK, A_KEY), BF16),
                   jax.ShapeDtypeStruct((b, s // Q_BLOCK, IDX_HEADS, Q_BLOCK, IDX_DIM), BF16)],
        compiler_params=_params("parallel", "parallel"),
    )(cq, w_q, cos_a, sin_a, cos_i, sin_i)


KEY_NEG_INF = -2139095041


def _key_to_float(key):
    return pltpu.bitcast(key ^ ((key >> 31) & jnp.int32(0x7FFFFFFF)), F32)


def _dsa_idx_kernel(qidx_ref, kidx_ref, widx_ref, bias_ref, score_scr, last_scr, *, topk, tk):
    qb = pl.program_id(1)
    s = kidx_ref.shape[1]
    n_tiles = ((qb + 1) * Q_BLOCK + tk - 1) // tk
    q_all = qidx_ref[0, 0].reshape(IDX_HEADS * Q_BLOCK, IDX_DIM)
    w = widx_ref[0]
    q_chunk = (qb * Q_BLOCK + lax.broadcasted_iota(jnp.int32, (tk, Q_BLOCK), 1)) // CHUNK
    key0 = lax.broadcasted_iota(jnp.int32, (tk, Q_BLOCK), 0)
    sub = min(tk, 128)

    def tile_off(t):
        return pl.multiple_of(t * tk, tk)

    def allowed_tile(off):
        return ((key0 + off) // CHUNK) <= q_chunk

    def score_tile(t, carry):
        off = tile_off(t)
        allowed = allowed_tile(off)
        for j in range(tk // sub):
            rows = slice(j * sub, (j + 1) * sub)
            rel = _dot(kidx_ref[0, pl.ds(off + j * sub, sub), :], q_all, NT_DIMS)
            sc = jnp.zeros((sub, Q_BLOCK), F32)
            for h in range(IDX_HEADS):
                sc = sc + jnp.maximum(rel[:, h * Q_BLOCK:(h + 1) * Q_BLOCK], 0.0) * w[h:h + 1, :]
            score_scr[pl.ds(off + j * sub, sub), :] = jnp.where(allowed[rows], sc, -jnp.inf)
        return carry

    lax.fori_loop(0, n_tiles, score_tile, 0)

    cnt_rows = 64

    def count(pred):
        def body(t, acc):
            off = tile_off(t)
            hit = jnp.where(pred(score_scr[pl.ds(off, tk), :], off), 1.0, 0.0)
            for j in range(tk // cnt_rows):
                acc = acc + hit[j * cnt_rows:(j + 1) * cnt_rows]
            return acc

        acc = lax.fori_loop(0, n_tiles, body, jnp.zeros((cnt_rows, Q_BLOCK), F32))
        return jnp.sum(acc, axis=0, keepdims=True)

    kf = jnp.float32(topk)
    n_visited = (n_tiles * tk).astype(F32)

    def count_key(cand):
        cand_f = _key_to_float(cand)
        return jnp.where(cand < KEY_NEG_INF, n_visited, count(lambda sc, off: sc >= cand_f))

    thr = jnp.where(count_key(jnp.zeros((1, Q_BLOCK), jnp.int32)) >= kf, jnp.int32(0), jnp.int32(-2 ** 31))

    def bit_step(i, thr):
        cand = thr | lax.shift_left(jnp.int32(1), jnp.int32(30) - i)
        return jnp.where(count_key(cand) >= kf, cand, thr)

    thr = lax.fori_loop(0, 31, bit_step, thr)
    thr_f = _key_to_float(thr)
    take_all = thr < KEY_NEG_INF

    last_scr[...] = jnp.full((1, Q_BLOCK), s - 1, jnp.int32)
    n_ge = jnp.where(take_all, kf, count(lambda sc, off: sc >= thr_f))

    @pl.when(jnp.max(n_ge) > kf)
    def _():
        need = kf - count(lambda sc, off: sc > thr_f)

        def idx_step(i, last):
            cand = last | lax.shift_left(jnp.int32(1), jnp.int32((s - 1).bit_length() - 1) - i)
            below = count(lambda sc, off: (sc == thr_f) & (key0 + off < cand))
            return jnp.where(below < need, cand, last)

        last = lax.fori_loop(0, (s - 1).bit_length(), idx_step, jnp.zeros((1, Q_BLOCK), jnp.int32))
        last_scr[...] = jnp.where(n_ge > kf, last, s - 1)

    last = last_scr[...]

    def write_tile(t, carry):
        off = tile_off(t)
        sc = score_scr[pl.ds(off, tk), :]
        sel = ((sc > thr_f) | ((sc == thr_f) & (key0 + off <= last)) | take_all) & allowed_tile(off)
        bias_ref[0, :, pl.ds(off, tk)] = jnp.where(sel, 0.0, MASK_VALUE).T.astype(BF16)
        return carry

    lax.fori_loop(0, n_tiles, write_tile, 0)

    def mask_tile(t, carry):
        bias_ref[0, :, pl.ds(tile_off(t), tk)] = jnp.full((Q_BLOCK, tk), MASK_VALUE, BF16)
        return carry

    lax.fori_loop(n_tiles, s // tk, mask_tile, 0)


def _dsa_idx(qidx, kidx, widx_t, topk, tk=256):
    b, nb = qidx.shape[:2]
    s = kidx.shape[1]
    tk = min(tk, s)
    kern = functools.partial(_dsa_idx_kernel, topk=topk, tk=tk)
    return pl.pallas_call(
        kern,
        grid=(b, nb),
        in_specs=[pl.BlockSpec((1, 1, IDX_HEADS, Q_BLOCK, IDX_DIM), lambda bi, i: (bi, i, 0, 0, 0)),
                  pl.BlockSpec((1, s, IDX_DIM), lambda bi, i: (bi, 0, 0)),
                  pl.BlockSpec((1, IDX_HEADS, Q_BLOCK), lambda bi, i: (bi, 0, i))],
        out_specs=pl.BlockSpec((1, Q_BLOCK, s), lambda bi, i: (bi, i, 0)),
        out_shape=jax.ShapeDtypeStruct((b, s, s), BF16),
        scratch_shapes=[pltpu.VMEM((s, Q_BLOCK), F32), pltpu.VMEM((1, Q_BLOCK), jnp.int32)],
        compiler_params=_params("parallel", "parallel"),
    )(qidx, kidx, widx_t)


def _dsa_attn_kernel(q_ref, keys_ref, bias_ref, o_ref, m_scr, l_scr, acc_scr, *, tk, n_strips):
    qb = pl.program_id(1)
    hs = A_HEADS // n_strips
    rows = hs * Q_BLOCK
    m_scr[...] = jnp.full_like(m_scr, -jnp.inf)
    l_scr[...] = jnp.zeros_like(l_scr)
    acc_scr[...] = jnp.zeros_like(acc_scr)
    n_tiles = ((qb + 1) * Q_BLOCK + tk - 1) // tk

    def body(t, carry):
        off = pl.multiple_of(t * tk, tk)
        kt = keys_ref[0, pl.ds(off, tk), :]
        kv = kt[:, :A_KV_RANK]
        bias = bias_ref[0, :, pl.ds(off, tk)].astype(F32)

        def scores(i):
            q = q_ref[0, 0, i * hs:(i + 1) * hs].reshape(rows, A_KEY)
            return _dot(q, kt, NT_DIMS)

        def softmax_update(i, sc):
            rs = slice(i * rows, (i + 1) * rows)
            sc = (sc.reshape(hs, Q_BLOCK, tk) + bias[None]).reshape(rows, tk)
            m_prev = m_scr[rs]
            m_new = jnp.maximum(m_prev, jnp.max(sc, axis=1, keepdims=True))
            alpha = jnp.exp(m_prev - m_new)
            p = jnp.exp(sc - jnp.tile(m_new, (1, tk // 128)))
            l_scr[rs] = alpha * l_scr[rs] + jnp.sum(p, axis=1, keepdims=True)
            m_scr[rs] = m_new
            return p.astype(BF16), alpha

        def values(i, p, alpha):
            rs = slice(i * rows, (i + 1) * rows)
            acc_scr[rs] = acc_scr[rs] * jnp.tile(alpha, (1, A_KV_RANK // 128)) + _dot(p, kv)

        sc = {0: scores(0)}
        pa = {}
        for i in range(n_strips):
            if i + 1 < n_strips:
                sc[i + 1] = scores(i + 1)
            pa[i] = softmax_update(i, sc.pop(i))
            if i > 0:
                values(i - 1, *pa.pop(i - 1))
        values(n_strips - 1, *pa.pop(n_strips - 1))
        return carry

    lax.fori_loop(0, n_tiles, body, 0)
    o = acc_scr[...] / jnp.tile(l_scr[...], (1, A_KV_RANK // 128))
    o_ref[0, 0] = o.astype(BF16).reshape(A_HEADS, Q_BLOCK, A_KV_RANK)


def _dsa_attn(qatt, keys, bias, tk=512, n_strips=8):
    b, nb = qatt.shape[:2]
    s = keys.shape[1]
    tk = min(tk, s)
    rows = A_HEADS * Q_BLOCK
    kern = functools.partial(_dsa_attn_kernel, tk=tk, n_strips=n_strips)
    return pl.pallas_call(
        kern,
        grid=(b, nb),
        in_specs=[pl.BlockSpec((1, 1, A_HEADS, Q_BLOCK, A_KEY), lambda bi, i: (bi, i, 0, 0, 0)),
                  pl.BlockSpec((1, s, A_KEY), lambda bi, i: (bi, 0, 0)),
                  pl.BlockSpec((1, Q_BLOCK, s), lambda bi, i: (bi, i, 0))],
        out_specs=pl.BlockSpec((1, 1, A_HEADS, Q_BLOCK, A_KV_RANK), lambda bi, i: (bi, i, 0, 0, 0)),
        out_shape=jax.ShapeDtypeStruct((b, nb, A_HEADS, Q_BLOCK, A_KV_RANK), BF16),
        scratch_shapes=[pltpu.VMEM((rows, 128), F32), pltpu.VMEM((rows, 128), F32),
                        pltpu.VMEM((rows, A_KV_RANK), F32)],
        compiler_params=_params("parallel", "parallel"),
    )(qatt, keys, bias)


def _dsa_out_kernel(ol_ref, wuv_ref, wo_ref, x_ref, g_ref, o_ref, y_scr):
    nb = ol_ref.shape[1]
    tm = nb * Q_BLOCK
    for h in range(A_HEADS):
        oh = ol_ref[0, :, h].reshape(tm, A_KV_RANK)
        y_scr[:, h * A_V:(h + 1) * A_V] = _dot(oh, wuv_ref[h]).astype(BF16)
    o_ref[0] = x_ref[0] + g_ref[0] * _dot(y_scr[...], wo_ref[...])


def _dsa_out(olat, w_uv_h, w_o, x, g, tm=512):
    b, s, d = x.shape
    nb = tm // Q_BLOCK
    return pl.pallas_call(
        _dsa_out_kernel,
        grid=(b, s // tm),
        in_specs=[pl.BlockSpec((1, nb, A_HEADS, Q_BLOCK, A_KV_RANK), lambda bi, i: (bi, i, 0, 0, 0)),
                  pl.BlockSpec(w_uv_h.shape, lambda bi, i: (0, 0, 0)),
                  pl.BlockSpec(w_o.shape, lambda bi, i: (0, 0)),
                  pl.BlockSpec((1, tm, d), lambda bi, i: (bi, i, 0)),
                  pl.BlockSpec((1, 1, d), lambda bi, i: (bi, 0, 0))],
        out_specs=pl.BlockSpec((1, tm, d), lambda bi, i: (bi, i, 0)),
        out_shape=jax.ShapeDtypeStruct((b, s, d), F32),
        scratch_shapes=[pltpu.VMEM((tm, A_HEADS * A_V), BF16)],
        compiler_params=_params("parallel", "parallel"),
    )(olat, w_uv_h, w_o, x, g)


def _rope_tables(positions):
    def angles(dim):
        inv = 1.0 / (ROPE_BASE ** (jnp.arange(0, dim, 2, dtype=F32) / dim))
        ang = positions.astype(F32)[..., None] * inv
        return jnp.cos(ang), jnp.sin(ang)

    cos_a, sin_a = angles(A_ROPE)
    cos_i, sin_i = angles(IDX_ROPE)
    return cos_a, sin_a, cos_i, sin_i


def _dsa_mixer(x, sc, sh, g, positions, w_in, qg, kvg, w_uq, w_qidx, lng, lnb, w_uk, w_uv, w_o):
    b, s, d = x.shape
    topk = min(TOPK_MAX, s // 4)
    cos_a, sin_a, cos_i, sin_i = _rope_tables(positions)
    w_in_p = jnp.pad(w_in, ((0, 0), (0, A_IN_PAD - A_IN))).astype(BF16)
    cq, keys, kidx, widx = _dsa_in(x, sc, sh, w_in_p, qg.reshape(1, -1), kvg.reshape(1, -1),
                                   lng.reshape(1, -1), lnb.reshape(1, -1), cos_a, sin_a, cos_i, sin_i)

    rq = w_uq.shape[0]
    w_uq_h = w_uq.reshape(rq, A_HEADS, A_NOPE + A_ROPE)
    w_lat = _fold_uq_uk(jnp.transpose(w_uq_h[:, :, :A_NOPE], (1, 0, 2)), jnp.transpose(w_uk, (1, 2, 0)))
    w_lat = jnp.transpose(w_lat, (1, 0, 2)).reshape(rq, A_HEADS * A_KV_RANK)
    w_r = w_uq_h[:, :, A_NOPE:]
    w_rs = jnp.concatenate([w_r[..., A_ROPE // 2:], w_r[..., :A_ROPE // 2]], axis=-1)
    w_i = w_qidx.reshape(rq, IDX_HEADS, IDX_DIM)
    hr = IDX_ROPE // 2
    w_is = jnp.concatenate([w_i[..., hr:IDX_ROPE], w_i[..., :hr], jnp.zeros_like(w_i[..., IDX_ROPE:])], axis=-1)
    w_q = jnp.concatenate([w_lat, w_r.reshape(rq, -1).astype(BF16), w_rs.reshape(rq, -1).astype(BF16),
                           w_i.reshape(rq, -1).astype(BF16), w_is.reshape(rq, -1).astype(BF16)], axis=1)
    qatt, qidx = _dsa_q(cq, w_q, cos_a, sin_a, cos_i, sin_i)

    bias = _dsa_idx(qidx, kidx, jnp.transpose(widx, (0, 2, 1)), topk)
    olat = _dsa_attn(qatt, keys, bias)
    w_uv_h = jnp.transpose(w_uv, (1, 0, 2)).astype(BF16)
    return _dsa_out(olat, w_uv_h, w_o.astype(BF16), x, g)


def _rwkv_in_kernel(x_ref, xp_ref, sc_ref, sh_ref, mu_ref, wl1_ref, wr_ref, wk_ref, wv_ref,
                    w2_ref, a2_ref, g2_ref, vec_ref, seg_ref,
                    r_out, lw_out, k_out, v_out, kk_out, a_out, g_out, bonus_out,
                    mix_scr, lora_scr):
    i = pl.program_id(1)
    j = pl.program_id(2)

    @pl.when(j == 0)
    def _():
        sc = sc_ref[0]
        sh = sh_ref[0]
        h = _rms_mod(x_ref[0], sc, sh)
        hp = _rms_mod(xp_ref[0, 7:8, :], sc, sh)
        hp = jnp.where(i == 0, 0.0, hp)
        rows = lax.broadcasted_iota(jnp.int32, h.shape, 0)
        h_prev = jnp.where(rows == 0, hp, pltpu.roll(h, 1, axis=0))
        delta = h_prev - h
        for m in range(6):
            mix_scr[m] = (h + delta * mu_ref[m:m + 1, :]).astype(BF16)
        tw = jnp.tanh(_dot(mix_scr[1], wl1_ref[:, :LORA_PAD]))
        ta = _dot(mix_scr[4], wl1_ref[:, LORA_PAD:2 * LORA_PAD])
        tg = _sigmoid(_dot(mix_scr[5], wl1_ref[:, 2 * LORA_PAD:]))
        lora_scr[:, :LORA_PAD] = tw.astype(BF16)
        lora_scr[:, LORA_PAD:2 * LORA_PAD] = ta.astype(BF16)
        lora_scr[:, 2 * LORA_PAD:] = tg.astype(BF16)

    w0 = vec_ref[0:1, :]
    a0 = vec_ref[1:2, :]
    k_k = vec_ref[2:3, :]
    k_a = vec_ref[3:4, :]
    r_k = vec_ref[4:5, :]
    seg = seg_ref[...]
    r = _dot(mix_scr[0], wr_ref[...])
    k = _dot(mix_scr[2], wk_ref[...])
    v = _dot(mix_scr[3], wv_ref[...])
    z = -(w0 + _dot(lora_scr[:, :LORA_PAD], w2_ref[...]))
    softplus = jnp.maximum(z, 0.0) + jnp.log(1.0 + jnp.exp(-jnp.abs(z)))
    lw = -jnp.exp(-softplus - 0.5)
    a = _sigmoid(a0 + _dot(lora_scr[:, LORA_PAD:2 * LORA_PAD], a2_ref[...]))
    g = _dot(lora_scr[:, 2 * LORA_PAD:], g2_ref[...])
    kk = k * k_k
    nrm = jnp.sqrt(_segsum(kk * kk, seg))
    kk = kk / jnp.maximum(nrm, 1e-12)
    k2 = k * (1.0 + (a - 1.0) * k_a)
    bonus = _segsum(r * k2 * r_k, seg) * v
    r_out[0] = r.astype(BF16)
    lw_out[0] = lw
    k_out[0] = k2.astype(BF16)
    v_out[0] = v.astype(BF16)
    kk_out[0] = kk.astype(BF16)
    a_out[0] = a.astype(BF16)
    g_out[0] = g.astype(BF16)
    bonus_out[0] = bonus.astype(BF16)


def _rwkv_in(x, sc, sh, mu, wl1, w_r, w_k, w_v, w2, a2, g2, vec, seg, tm=512, tn=256):
    b, s, d = x.shape
    sub = 8
    vecs = pl.BlockSpec((1, 1, d), lambda bi, i, j: (bi, 0, 0))
    col = lambda rows: pl.BlockSpec((rows, tn), lambda bi, i, j: (0, j))
    out = pl.BlockSpec((1, tm, tn), lambda bi, i, j: (bi, i, j))
    return pl.pallas_call(
        _rwkv_in_kernel,
        grid=(b, s // tm, d // tn),
        in_specs=[pl.BlockSpec((1, tm, d), lambda bi, i, j: (bi, i, 0)),
                  pl.BlockSpec((1, sub, d), lambda bi, i, j: (bi, jnp.maximum(i * (tm // sub) - 1, 0), 0)),
                  vecs, vecs,
                  pl.BlockSpec(mu.shape, lambda bi, i, j: (0, 0)),
                  pl.BlockSpec(wl1.shape, lambda bi, i, j: (0, 0)),
                  col(d), col(d), col(d), col(LORA_PAD), col(LORA_PAD), col(R_GATE_LORA),
                  col(8), pl.BlockSpec(seg.shape, lambda bi, i, j: (0, 0))],
        out_specs=[out] * 8,
        out_shape=[jax.ShapeDtypeStruct((b, s, d), F32 if i == 1 else BF16) for i in range(8)],
        scratch_shapes=[pltpu.VMEM((6, tm, d), BF16), pltpu.VMEM((tm, 2 * LORA_PAD + R_GATE_LORA), BF16)],
        compiler_params=_params("parallel", "parallel", "arbitrary"),
    )(x, x, sc, sh, mu, wl1, w_r, w_k, w_v, w2, a2, g2, vec, seg)


def _rwkv_rec_kernel(r_ref, lw_ref, k_ref, v_ref, kk_ref, a_ref, o_ref, s_scr, *, n_chunks, n_heads):
    t = pl.program_id(2)

    @pl.when(t == 0)
    def _():
        s_scr[...] = jnp.zeros_like(s_scr)

    c_len = CHUNK
    n = R_HEAD
    ri = lax.broadcasted_iota(jnp.int32, (c_len, c_len), 0)
    ci = lax.broadcasted_iota(jnp.int32, (c_len, c_len), 1)
    strict = ri > ci
    incl = ri >= ci
    tri = jnp.where(incl, 1.0, 0.0).astype(BF16)
    items = [(c, h) for c in range(n_chunks) for h in range(n_heads)]

    pre = []
    for c in range(n_chunks):
        sl = slice(c * c_len, (c + 1) * c_len)
        lw = lw_ref[0, sl, :]
        k = k_ref[0, sl, :].astype(F32)
        kk = kk_ref[0, sl, :].astype(F32)
        kb = kk * a_ref[0, sl, :].astype(F32)
        cum = _cumsum_rows(tri, lw)
        cum_c = cum[c_len - 1:c_len, :]
        p_inv = jnp.exp(-cum)
        p_hat = jnp.exp(cum_c - cum)
        pre.append(dict(
            r_t=r_ref[0, sl, :].astype(F32) * jnp.exp(cum), a_t=-kk * jnp.exp(cum - lw), b_t=kb * p_inv,
            k_t=k * p_inv, b_h=kb * p_hat, k_h=k * p_hat, p_c=jnp.exp(cum_c), v=v_ref[0, sl, :].astype(F32)))

    def head(c, h, name):
        return pre[c][name][:, h * n:(h + 1) * n]

    amat = {}
    for c, h in items:
        ar = jnp.concatenate([head(c, h, "a_t"), head(c, h, "r_t")], axis=0)
        bk = jnp.concatenate([head(c, h, "b_t"), head(c, h, "k_t")], axis=0)
        amat[c, h] = _mm(ar, bk, NT_DIMS, PASSES_PAIR)
    l_ab, m_rb, lv = {}, {}, {}
    for c, h in items:
        am = amat[c, h]
        l_ab[c, h] = jnp.where(strict, am[:c_len, :c_len], 0.0)
        m_rb[c, h] = jnp.where(incl, am[c_len:, :c_len], 0.0)
        lm = jnp.concatenate([jnp.where(strict, am[:c_len, c_len:], 0.0),
                              jnp.where(incl, am[c_len:, c_len:], 0.0)], axis=0)
        lv[c, h] = _mm(lm, head(c, h, "v"), NN_DIMS, PASSES_PAIR)
    z = {}
    for c, h in items:
        z[c, h] = jnp.concatenate([head(c, h, "a_t"), lv[c, h][:c_len], l_ab[c, h]], axis=1)
    for it in range(6):
        for c, h in items:
            zz = z[c, h]
            pw = zz[:, 2 * n:]
            if it < 5:
                x = _mm(pw, zz, NN_DIMS, PASSES_SOLVE)
                z[c, h] = jnp.concatenate([zz[:, :2 * n] + x[:, :2 * n], x[:, 2 * n:]], axis=1)
            else:
                z[c, h] = zz[:, :2 * n] + _mm(pw, zz[:, :2 * n], NN_DIMS, PASSES_SOLVE)
    state = [s_scr[h] for h in range(n_heads)]
    for c in range(n_chunks):
        u = [_mm(z[c, h][:, :n], state[h], NT_DIMS, PASSES_STATE) + z[c, h][:, n:] for h in range(n_heads)]
        o = [_mm(head(c, h, "r_t"), state[h], NT_DIMS, PASSES_STATE) + _mm(m_rb[c, h], u[h], NN_DIMS, PASSES_STATE)
             + lv[c, h][c_len:] for h in range(n_heads)]
        state = [state[h] * head(c, h, "p_c")
                 + _mm(u[h], head(c, h, "b_h"), TN_DIMS, PASSES_STATE)
                 + _mm(head(c, h, "v"), head(c, h, "k_h"), TN_DIMS, PASSES_STATE) for h in range(n_heads)]
        o_ref[0, c * c_len:(c + 1) * c_len, :] = jnp.concatenate(o, axis=1)
    for h in range(n_heads):
        s_scr[h] = state[h]


def _mm(a, b, dims=NN_DIMS, passes=1):
    if passes == 1:
        return _dot(a.astype(BF16), b.astype(BF16), dims)
    return _dot3(a, b, dims)


def _cumsum_rows(tri, x):
    hi, lo = _split(x)
    lo2 = (x - hi.astype(F32) - lo.astype(F32)).astype(BF16)
    return _dot(tri, hi) + _dot(tri, lo) + _dot(tri, lo2)


def _rwkv_rec(r, lw, k, v, kk, a, tb=256, lanes=512):
    b, s, d = r.shape
    tb = min(tb, s)
    n_heads = lanes // R_HEAD
    kern = functools.partial(_rwkv_rec_kernel, n_chunks=tb // CHUNK, n_heads=n_heads)
    blk = pl.BlockSpec((1, tb, lanes), lambda bi, hi, ti: (bi, ti, hi))
    return pl.pallas_call(
        kern,
        grid=(b, d // lanes, s // tb),
        in_specs=[blk] * 6,
        out_specs=blk,
        out_shape=jax.ShapeDtypeStruct((b, s, d), F32),
        scratch_shapes=[pltpu.VMEM((n_heads, R_HEAD, R_HEAD), F32)],
        compiler_params=_params("parallel", "parallel", "arbitrary"),
    )(r, lw, k, v, kk, a)


def _rwkv_out_kernel(o_ref, g_ref, bonus_ref, x_ref, gate_ref, gng_ref, gnb_ref, seg_ref, wo_ref,
                     out_ref, y_scr):
    d = x_ref.shape[2]
    tn = seg_ref.shape[0]
    seg = seg_ref[...]
    inv_n = 1.0 / R_HEAD
    for jt in range(d // tn):
        sl = slice(jt * tn, (jt + 1) * tn)
        o = o_ref[0, :, sl]
        mu = _segsum(o, seg) * inv_n
        dv = o - mu
        var = _segsum(dv * dv, seg) * inv_n
        ln = dv * lax.rsqrt(var + R_GN_EPS)
        y = ((ln * gng_ref[:, sl] + gnb_ref[:, sl] + bonus_ref[0, :, sl].astype(F32))
             * g_ref[0, :, sl].astype(F32))
        y_scr[:, sl] = y.astype(BF16)
    out_ref[0] = x_ref[0] + gate_ref[0] * _dot(y_scr[...], wo_ref[...])


def _rwkv_out(o, g, bonus, x, gate, gng, gnb, seg, w_o, tm=512):
    b, s, d = x.shape
    tok = pl.BlockSpec((1, tm, d), lambda bi, i: (bi, i, 0))
    full = lambda a: pl.BlockSpec(a.shape, lambda bi, i: (0,) * a.ndim)
    return pl.pallas_call(
        _rwkv_out_kernel,
        grid=(b, s // tm),
        in_specs=[tok, tok, tok, tok, pl.BlockSpec((1, 1, d), lambda bi, i: (bi, 0, 0)),
                  full(gng), full(gnb), full(seg), full(w_o)],
        out_specs=tok,
        out_shape=jax.ShapeDtypeStruct((b, s, d), F32),
        scratch_shapes=[pltpu.VMEM((tm, d), BF16)],
        compiler_params=_params("parallel", "parallel"),
    )(o, g, bonus, x, gate, gng, gnb, seg, w_o)


def _rwkv_mixer(x, sc, sh, gate, mu, w_r, w_k, w_v, w_o, w0, w_w1, w_w2, a0, w_a1, w_a2, w_g1, w_g2,
                k_k, k_a, r_k, gn_g, gn_b):
    d = x.shape[2]
    tn = 256
    lora = w_w1.shape[1]
    padc = lambda w: jnp.pad(w, ((0, 0), (0, LORA_PAD - lora)))
    padr = lambda w: jnp.pad(w, ((0, LORA_PAD - lora), (0, 0)))
    wl1 = jnp.concatenate([padc(w_w1), padc(w_a1), w_g1], axis=1).astype(BF16)
    vec = jnp.stack([w0, a0, k_k, k_a, r_k.reshape(-1), w0 * 0, w0 * 0, w0 * 0])
    idx = jnp.arange(tn) // R_HEAD
    seg = (idx[:, None] == idx[None, :]).astype(BF16)
    r, lw, k2, v, kk, a, g, bonus = _rwkv_in(
        x, sc, sh, mu, wl1, w_r.astype(BF16), w_k.astype(BF16), w_v.astype(BF16),
        padr(w_w2).astype(BF16), padr(w_a2).astype(BF16), w_g2.astype(BF16), vec, seg, tn=tn)
    o = _rwkv_rec(r, lw, k2, v, kk, a)
    return _rwkv_out(o, g, bonus, x, gate, gn_g.reshape(1, d), gn_b.reshape(1, d), seg, w_o.astype(BF16))


def kernel(x, c, positions, ada_w, ada_b, mlp_w1, mlp_w2, final_g, a_w_in, a_q_norm_g, a_kv_norm_g, a_w_uq,
           a_w_qidx, a_kidx_ln_g, a_kidx_ln_b, a_w_uk, a_w_uv, a_w_o, b_mu, b_w_r, b_w_k, b_w_v, b_w_o, b_w0,
           b_w_w1, b_w_w2, b_a0, b_w_a1, b_w_a2, b_w_g1, b_w_g2, b_k_k, b_k_a, b_r_k, b_gn_g, b_gn_b):
    depth = ada_w.shape[0]
    d = x.shape[2]
    mod = _ada_mod(c, ada_w, ada_b)
    for i in range(depth):
        sh1, sc1, g1, sh2, sc2, g2 = [mod[i, :, None, m * d:(m + 1) * d] for m in range(6)]
        j = i // 2
        if i % 2 == 0:
            x = _dsa_mixer(x, sc1, sh1, g1, positions, a_w_in[j], a_q_norm_g[j], a_kv_norm_g[j], a_w_uq[j],
                           a_w_qidx[j], a_kidx_ln_g[j], a_kidx_ln_b[j], a_w_uk[j], a_w_uv[j], a_w_o[j])
        else:
            x = _rwkv_mixer(x, sc1, sh1, g1, b_mu[j], b_w_r[j], b_w_k[j], b_w_v[j], b_w_o[j], b_w0[j],
                            b_w_w1[j], b_w_w2[j], b_a0[j], b_w_a1[j], b_w_a2[j], b_w_g1[j], b_w_g2[j],
                            b_k_k[j], b_k_a[j], b_r_k[j], b_gn_g[j], b_gn_b[j])
        x = _mlp(x, sc2, sh2, g2, mlp_w1[i].astype(BF16), mlp_w2[i].astype(BF16), final_g,
                 final_norm=(i == depth - 1))
    return x
```

```python
import functools

import jax
import jax.numpy as jnp
from jax import lax
from jax.experimental import pallas as pl
from jax.experimental.pallas import tpu as pltpu

F32 = jnp.float32
BF16 = jnp.bfloat16

D_MODEL = 2048
DEPTH = 2
CHUNK = 64
EPS = 1e-6
A_HEADS = 16
A_NOPE = 128
A_ROPE = 64
A_V = 128
A_Q_RANK = 512
A_KV_RANK = 256
A_KEY = A_KV_RANK + A_ROPE
IDX_HEADS = 16
IDX_DIM = 64
IDX_ROPE = 32
TOPK_MAX = 256
Q_BLOCK = 128
ROPE_BASE = 10000.0
A_SCALE = (A_NOPE + A_ROPE) ** -0.5
A_IN = A_Q_RANK + A_KV_RANK + A_ROPE + IDX_DIM + IDX_HEADS
A_IN_PAD = 1024
R_HEAD = 64
R_GN_EPS = R_HEAD * 1e-5
LORA_PAD = 128
R_GATE_LORA = 256

VMEM_LIMIT_BYTES = 56 * 1024 * 1024
MASK_VALUE = -1e30

NT_DIMS = (((1,), (1,)), ((), ()))
TN_DIMS = (((0,), (0,)), ((), ()))
NN_DIMS = (((1,), (0,)), ((), ()))

PASSES_PAIR = 1
PASSES_SOLVE = 1
PASSES_STATE = 1


def _params(*sem):
    return pltpu.CompilerParams(dimension_semantics=sem, vmem_limit_bytes=VMEM_LIMIT_BYTES)


def _dot(a, b, dims=NN_DIMS):
    return lax.dot_general(a, b, dims, preferred_element_type=F32)


def _split(x):
    hi = x.astype(BF16)
    lo = (x - hi.astype(F32)).astype(BF16)
    return hi, lo


def _dot3(a, b, dims=NN_DIMS):
    ah, al = _split(a)
    bh, bl = _split(b)
    return _dot(ah, bh, dims) + _dot(ah, bl, dims) + _dot(al, bh, dims)


def _segsum(x, seg):
    return _dot(x.astype(BF16), seg)


def _col_tiles(w, tn):
    k, n = w.shape
    return jnp.transpose(w.reshape(k, n // tn, tn), (1, 0, 2))


def _rms(x):
    return x * lax.rsqrt(jnp.mean(x * x, axis=-1, keepdims=True) + EPS)


def _rms_mod(x, sc, sh):
    return _rms(x) * (1.0 + sc) + sh


def _sigmoid(x):
    return 1.0 / (1.0 + jnp.exp(-x))


def _rope_half(x, c, s):
    d = x.shape[-1] // 2
    x1 = x[:, :d]
    x2 = x[:, d:]
    return jnp.concatenate([x1 * c - x2 * s, x2 * c + x1 * s], axis=-1)


def _mod_kernel(c_ref, w_ref, b_ref, o_ref):
    c = c_ref[...]
    a = (c * _sigmoid(c)).astype(BF16)
    o_ref[0] = _dot(a, w_ref[0].astype(BF16)) + b_ref[0]


def _ada_mod(c, ada_w, ada_b):
    depth, d, n = ada_w.shape
    b = c.shape[0]
    tn = 1024
    return pl.pallas_call(
        _mod_kernel,
        grid=(depth, n // tn),
        in_specs=[
            pl.BlockSpec((b, d), lambda l, j: (0, 0)),
            pl.BlockSpec((1, d, tn), lambda l, j: (l, 0, j)),
            pl.BlockSpec((1, 1, tn), lambda l, j: (l, 0, j)),
        ],
        out_specs=pl.BlockSpec((1, b, tn), lambda l, j: (l, 0, j)),
        out_shape=jax.ShapeDtypeStruct((depth, b, n), F32),
        compiler_params=_params("arbitrary", "arbitrary"),
    )(c, ada_w, ada_b.reshape(depth, 1, n))


def _mlp_kernel(x_ref, xn_ref, sc_ref, sh_ref, scn_ref, shn_ref, g_ref, w1_ref, w2_ref, fg_ref, o_ref,
                h_scr, acc_scr, *, final_norm):
    f = pl.program_id(2)
    last = pl.num_programs(2) - 1

    @pl.when((pl.program_id(0) == 0) & (pl.program_id(1) == 0) & (f == 0))
    def _():
        h_scr[...] = _rms_mod(x_ref[0], sc_ref[0], sh_ref[0]).astype(BF16)
        acc_scr[...] = jnp.zeros_like(acc_scr)

    def hidden():
        a = jnp.maximum(_dot(h_scr[...], w1_ref[0]), 0.0)
        return _dot((a * a).astype(BF16), w2_ref[...])

    @pl.when(f < last)
    def _():
        acc_scr[...] += hidden()

    @pl.when(f == last)
    def _():
        y = x_ref[0] + g_ref[0] * (acc_scr[...] + hidden())
        if final_norm:
            y = _rms(y) * fg_ref[...]
        o_ref[0] = y
        h_scr[...] = _rms_mod(xn_ref[0], scn_ref[0], shn_ref[0]).astype(BF16)
        acc_scr[...] = jnp.zeros_like(acc_scr)


def _mlp(x, sc, sh, g, w1, w2, final_g, final_norm, tm=512, tf=1024):
    b, s, d = x.shape
    dff = w1.shape[1]
    n_i = s // tm
    kern = functools.partial(_mlp_kernel, final_norm=final_norm)

    def next_block(bi, i):
        return jnp.minimum(bi * n_i + i + 1, b * n_i - 1)

    vec = pl.BlockSpec((1, 1, d), lambda bi, i, f: (bi, 0, 0))
    vec_next = pl.BlockSpec((1, 1, d), lambda bi, i, f: (next_block(bi, i) // n_i, 0, 0))
    return pl.pallas_call(
        kern,
        grid=(b, n_i, dff // tf),
        in_specs=[
            pl.BlockSpec((1, tm, d), lambda bi, i, f: (bi, i, 0)),
            pl.BlockSpec((1, tm, d), lambda bi, i, f: (next_block(bi, i) // n_i, next_block(bi, i) % n_i, 0)),
            vec, vec, vec_next, vec_next, vec,
            pl.BlockSpec((1, d, tf), lambda bi, i, f: (f, 0, 0)),
            pl.BlockSpec((tf, d), lambda bi, i, f: (f, 0)),
            pl.BlockSpec((1, d), lambda bi, i, f: (0, 0)),
        ],
        out_specs=pl.BlockSpec((1, tm, d), lambda bi, i, f: (bi, i, 0)),
        out_shape=jax.ShapeDtypeStruct((b, s, d), F32),
        scratch_shapes=[pltpu.VMEM((tm, d), BF16), pltpu.VMEM((tm, d), F32)],
        compiler_params=_params("arbitrary", "arbitrary", "arbitrary"),
    )(x, x, sc, sh, sc, sh, g, _col_tiles(w1, tf), w2, final_g.reshape(1, d))


def _dsa_in_kernel(x_ref, sc_ref, sh_ref, w_ref, qg_ref, kvg_ref, lng_ref, lnb_ref,
                   ca_ref, sa_ref, ci_ref, si_ref, cq_ref, keys_ref, kidx_ref, widx_ref):
    o_kv = A_Q_RANK
    o_kr = o_kv + A_KV_RANK
    o_ki = o_kr + A_ROPE
    o_wi = o_ki + IDX_DIM
    tm = x_ref.shape[1]
    halves = [slice(0, tm // 2), slice(tm // 2, tm)]
    hs = [_rms_mod(x_ref[0, rs], sc_ref[0], sh_ref[0]).astype(BF16) for rs in halves]
    projs = [_dot(h, w_ref[...]) for h in hs]
    for rs, proj in zip(halves, projs):
        cq_ref[0, rs] = (_rms(proj[:, :o_kv]) * qg_ref[...]).astype(BF16)
        keys_ref[0, rs, :A_KV_RANK] = (_rms(proj[:, o_kv:o_kr]) * kvg_ref[...]).astype(BF16)
        keys_ref[0, rs, A_KV_RANK:] = _rope_half(proj[:, o_kr:o_ki], ca_ref[0, rs], sa_ref[0, rs]).astype(BF16)
        ki = proj[:, o_ki:o_wi]
        mu = jnp.mean(ki, axis=-1, keepdims=True)
        kc = ki - mu
        var = jnp.mean(kc * kc, axis=-1, keepdims=True)
        ki = kc * lax.rsqrt(var + EPS) * lng_ref[...] + lnb_ref[...]
        ki = jnp.concatenate([_rope_half(ki[:, :IDX_ROPE], ci_ref[0, rs], si_ref[0, rs]), ki[:, IDX_ROPE:]],
                             axis=-1)
        kidx_ref[0, rs] = ki.astype(BF16)
        widx_ref[0, rs] = proj[:, o_wi:o_wi + IDX_HEADS] * (IDX_HEADS ** -0.5 * IDX_DIM ** -0.5)


def _dsa_in(x, sc, sh, w_in_p, qg, kvg, lng, lnb, cos_a, sin_a, cos_i, sin_i, tm=512):
    b, s, d = x.shape
    vec = pl.BlockSpec((1, 1, d), lambda bi, i: (bi, 0, 0))

    def full(a):
        return pl.BlockSpec(a.shape, lambda bi, i: (0,) * a.ndim)

    def tok(w):
        return pl.BlockSpec((1, tm, w), lambda bi, i: (bi, i, 0))

    return pl.pallas_call(
        _dsa_in_kernel,
        grid=(b, s // tm),
        in_specs=[tok(d), vec, vec, full(w_in_p), full(qg), full(kvg), full(lng), full(lnb),
                  tok(A_ROPE // 2), tok(A_ROPE // 2), tok(IDX_ROPE // 2), tok(IDX_ROPE // 2)],
        out_specs=[tok(A_Q_RANK), tok(A_KEY), tok(IDX_DIM), tok(IDX_HEADS)],
        out_shape=[jax.ShapeDtypeStruct((b, s, A_Q_RANK), BF16),
                   jax.ShapeDtypeStruct((b, s, A_KEY), BF16),
                   jax.ShapeDtypeStruct((b, s, IDX_DIM), BF16),
                   jax.ShapeDtypeStruct((b, s, IDX_HEADS), F32)],
        compiler_params=_params("parallel", "parallel"),
    )(x, sc, sh, w_in_p, qg, kvg, lng, lnb, cos_a, sin_a, cos_i, sin_i)


def _fold_kernel(a_ref, b_ref, o_ref):
    o_ref[0] = _dot3(a_ref[0], b_ref[0]).astype(BF16)


def _fold_uq_uk(w_uq_nope, w_uk_t):
    h, rq, dn = w_uq_nope.shape
    rkv = w_uk_t.shape[2]
    return pl.pallas_call(
        _fold_kernel,
        grid=(h,),
        in_specs=[pl.BlockSpec((1, rq, dn), lambda i: (i, 0, 0)),
                  pl.BlockSpec((1, dn, rkv), lambda i: (i, 0, 0))],
        out_specs=pl.BlockSpec((1, rq, rkv), lambda i: (i, 0, 0)),
        out_shape=jax.ShapeDtypeStruct((h, rq, rkv), BF16),
        compiler_params=_params("arbitrary"),
    )(w_uq_nope, w_uk_t)


def _dsa_q_kernel(cq_ref, w_ref, ca_ref, sa_ref, ci_ref, si_ref, qatt_ref, qidx_ref):
    cq = cq_ref[0]
    tm = cq.shape[0]
    ca, sa, ci, si = ca_ref[0], sa_ref[0], ci_ref[0], si_ref[0]
    one = jnp.ones((tm, IDX_DIM - IDX_ROPE), F32)
    cq_t = jnp.concatenate([ca, ca, ca, ca], axis=-1)
    sq_t = jnp.concatenate([-sa, sa, -sa, sa], axis=-1)
    ci_t = jnp.concatenate([ci, ci, one, ci, ci, one], axis=-1)
    si_t = jnp.concatenate([-si, si, 0.0 * one, -si, si, 0.0 * one], axis=-1)
    nb = tm // Q_BLOCK
    n_lat = A_HEADS * A_KV_RANK
    n_rope = A_HEADS * A_ROPE
    n_idx = IDX_HEADS * IDX_DIM
    for h in range(A_HEADS):
        lat = _dot(cq, w_ref[:, h * A_KV_RANK:(h + 1) * A_KV_RANK]) * A_SCALE
        qatt_ref[0, :, h, :, :A_KV_RANK] = lat.astype(BF16).reshape(nb, Q_BLOCK, A_KV_RANK)
    o = n_lat
    r = _dot(cq, w_ref[:, o:o + n_rope])
    rs = _dot(cq, w_ref[:, o + n_rope:o + 2 * n_rope])
    rep = n_rope // cq_t.shape[1]
    qr = ((r * jnp.tile(cq_t, (1, rep)) + rs * jnp.tile(sq_t, (1, rep))) * A_SCALE).astype(BF16)
    for h in range(A_HEADS):
        qatt_ref[0, :, h, :, A_KV_RANK:] = qr[:, h * A_ROPE:(h + 1) * A_ROPE].reshape(nb, Q_BLOCK, A_ROPE)
    o = n_lat + 2 * n_rope
    qi = _dot(cq, w_ref[:, o:o + n_idx])
    qis = _dot(cq, w_ref[:, o + n_idx:o + 2 * n_idx])
    rep = n_idx // ci_t.shape[1]
    qi = (qi * jnp.tile(ci_t, (1, rep)) + qis * jnp.tile(si_t, (1, rep))).astype(BF16)
    for h in range(IDX_HEADS):
        qidx_ref[0, :, h] = qi[:, h * IDX_DIM:(h + 1) * IDX_DIM].reshape(nb, Q_BLOCK, IDX_DIM)


def _dsa_q(cq, w_q, cos_a, sin_a, cos_i, sin_i, tm=512):
    b, s, rq = cq.shape
    nb = tm // Q_BLOCK

    def tok(w):
        return pl.BlockSpec((1, tm, w), lambda bi, i: (bi, i, 0))

    return pl.pallas_call(
        _dsa_q_kernel,
        grid=(b, s // tm),
        in_specs=[tok(rq), pl.BlockSpec(w_q.shape, lambda bi, i: (0, 0)),
                  tok(A_ROPE // 2), tok(A_ROPE // 2), tok(IDX_ROPE // 2), tok(IDX_ROPE // 2)],
        out_specs=[pl.BlockSpec((1, nb, A_HEADS, Q_BLOCK, A_KEY), lambda bi, i: (bi, i, 0, 0, 0)),
                   pl.BlockSpec((1, nb, IDX_HEADS, Q_BLOCK, IDX_DIM), lambda bi, i: (bi, i, 0, 0, 0))],
        out_shape=[jax.ShapeDtypeStruct((b, s // Q_BLOCK, A_HEADS, Q_BLOCK, A_KEY), BF16),
                   jax.ShapeDtypeStruct((b, s // Q_BLOCK, IDX_HEADS, Q_BLOCK, IDX_DIM), BF16)],
        compiler_params=_params("parallel", "parallel"),
    )(cq, w_q, cos_a, sin_a, cos_i, sin_i)


KEY_NEG_INF = -2139095041


def _key_to_float(key):
    return pltpu.bitcast(key ^ ((key >> 31) & jnp.int32(0x7FFFFFFF)), F32)


def _dsa_idx_kernel(qidx_ref, kidx_ref, widx_ref, bias_ref, score_scr, last_scr, *, topk, tk):
    qb = pl.program_id(1)
    s = kidx_ref.shape[1]
    n_tiles = ((qb + 1) * Q_BLOCK + tk - 1) // tk
    q_all = qidx_ref[0, 0].reshape(IDX_HEADS * Q_BLOCK, IDX_DIM)
    w = widx_ref[0]
    q_chunk = (qb * Q_BLOCK + lax.broadcasted_iota(jnp.int32, (tk, Q_BLOCK), 1)) // CHUNK
    key0 = lax.broadcasted_iota(jnp.int32, (tk, Q_BLOCK), 0)
    sub = min(tk, 128)

    def tile_off(t):
        return pl.multiple_of(t * tk, tk)

    def allowed_tile(off):
        return ((key0 + off) // CHUNK) <= q_chunk

    def score_tile(t, carry):
        off = tile_off(t)
        allowed = allowed_tile(off)
        for j in range(tk // sub):
            rows = slice(j * sub, (j + 1) * sub)
            rel = _dot(kidx_ref[0, pl.ds(off + j * sub, sub), :], q_all, NT_DIMS)
            sc = jnp.zeros((sub, Q_BLOCK), F32)
            for h in range(IDX_HEADS):
                sc = sc + jnp.maximum(rel[:, h * Q_BLOCK:(h + 1) * Q_BLOCK], 0.0) * w[h:h + 1, :]
            score_scr[pl.ds(off + j * sub, sub), :] = jnp.where(allowed[rows], sc, -jnp.inf)
        return carry

    lax.fori_loop(0, n_tiles, score_tile, 0)

    cnt_rows = 64

    def count(pred):
        def body(t, acc):
            off = tile_off(t)
            hit = jnp.where(pred(score_scr[pl.ds(off, tk), :], off), 1.0, 0.0)
            for j in range(tk // cnt_rows):
                acc = acc + hit[j * cnt_rows:(j + 1) * cnt_rows]
            return acc

        acc = lax.fori_loop(0, n_tiles, body, jnp.zeros((cnt_rows, Q_BLOCK), F32))
        return jnp.sum(acc, axis=0, keepdims=True)

    kf = jnp.float32(topk)
    n_visited = (n_tiles * tk).astype(F32)

    def count_key(cand):
        cand_f = _key_to_float(cand)
        return jnp.where(cand < KEY_NEG_INF, n_visited, count(lambda sc, off: sc >= cand_f))

    thr = jnp.where(count_key(jnp.zeros((1, Q_BLOCK), jnp.int32)) >= kf, jnp.int32(0), jnp.int32(-2 ** 31))

    def bit_step(i, thr):
        cand = thr | lax.shift_left(jnp.int32(1), jnp.int32(30) - i)
        return jnp.where(count_key(cand) >= kf, cand, thr)

    thr = lax.fori_loop(0, 31, bit_step, thr)
    thr_f = _key_to_float(thr)
    take_all = thr < KEY_NEG_INF

    last_scr[...] = jnp.full((1, Q_BLOCK), s - 1, jnp.int32)
    n_ge = jnp.where(take_all, kf, count(lambda sc, off: sc >= thr_f))

    @pl.when(jnp.max(n_ge) > kf)
    def _():
        need = kf - count(lambda sc, off: sc > thr_f)

        def idx_step(i, last):
            cand = last | lax.shift_left(jnp.int32(1), jnp.int32((s - 1).bit_length() - 1) - i)
            below = count(lambda sc, off: (sc == thr_f) & (key0 + off < cand))
            return jnp.where(below < need, cand, last)

        last = lax.fori_loop(0, (s - 1).bit_length(), idx_step, jnp.zeros((1, Q_BLOCK), jnp.int32))
        last_scr[...] = jnp.where(n_ge > kf, last, s - 1)

    last = last_scr[...]

    def write_tile(t, carry):
        off = tile_off(t)
        sc = score_scr[pl.ds(off, tk), :]
        sel = ((sc > thr_f) | ((sc == thr_f) & (key0 + off <= last)) | take_all) & allowed_tile(off)
        bias_ref[0, :, pl.ds(off, tk)] = jnp.where(sel, 0.0, MASK_VALUE).T.astype(BF16)
        return carry

    lax.fori_loop(0, n_tiles, write_tile, 0)

    def mask_tile(t, carry):
        bias_ref[0, :, pl.ds(tile_off(t), tk)] = jnp.full((Q_BLOCK, tk), MASK_VALUE, BF16)
        return carry

    lax.fori_loop(n_tiles, s // tk, mask_tile, 0)


def _dsa_idx(qidx, kidx, widx_t, topk, tk=256):
    b, nb = qidx.shape[:2]
    s = kidx.shape[1]
    tk = min(tk, s)
    kern = functools.partial(_dsa_idx_kernel, topk=topk, tk=tk)
    return pl.pallas_call(
        kern,
        grid=(b, nb),
        in_specs=[pl.BlockSpec((1, 1, IDX_HEADS, Q_BLOCK, IDX_DIM), lambda bi, i: (bi, i, 0, 0, 0)),
                  pl.BlockSpec((1, s, IDX_DIM), lambda bi, i: (bi, 0, 0)),
                  pl.BlockSpec((1, IDX_HEADS, Q_BLOCK), lambda bi, i: (bi, 0, i))],
        out_specs=pl.BlockSpec((1, Q_BLOCK, s), lambda bi, i: (bi, i, 0)),
        out_shape=jax.ShapeDtypeStruct((b, s, s), BF16),
        scratch_shapes=[pltpu.VMEM((s, Q_BLOCK), F32), pltpu.VMEM((1, Q_BLOCK), jnp.int32)],
        compiler_params=_params("parallel", "parallel"),
    )(qidx, kidx, widx_t)


def _dsa_attn_kernel(q_ref, keys_ref, bias_ref, o_ref, m_scr, l_scr, acc_scr, *, tk, n_strips):
    qb = pl.program_id(1)
    hs = A_HEADS // n_strips
    rows = hs * Q_BLOCK
    m_scr[...] = jnp.full_like(m_scr, -jnp.inf)
    l_scr[...] = jnp.zeros_like(l_scr)
    acc_scr[...] = jnp.zeros_like(acc_scr)
    n_tiles = ((qb + 1) * Q_BLOCK + tk - 1) // tk

    def body(t, carry):
        off = pl.multiple_of(t * tk, tk)
        kt = keys_ref[0, pl.ds(off, tk), :]
        kv = kt[:, :A_KV_RANK]
        bias = bias_ref[0, :, pl.ds(off, tk)].astype(F32)

        def scores(i):
            q = q_ref[0, 0, i * hs:(i + 1) * hs].reshape(rows, A_KEY)
            return _dot(q, kt, NT_DIMS)

        def softmax_update(i, sc):
            rs = slice(i * rows, (i + 1) * rows)
            sc = (sc.reshape(hs, Q_BLOCK, tk) + bias[None]).reshape(rows, tk)
            m_prev = m_scr[rs]
            m_new = jnp.maximum(m_prev, jnp.max(sc, axis=1, keepdims=True))
            alpha = jnp.exp(m_prev - m_new)
            p = jnp.exp(sc - jnp.tile(m_new, (1, tk // 128)))
            l_scr[rs] = alpha * l_scr[rs] + jnp.sum(p, axis=1, keepdims=True)
            m_scr[rs] = m_new
            return p.astype(BF16), alpha

        def values(i, p, alpha):
            rs = slice(i * rows, (i + 1) * rows)
            acc_scr[rs] = acc_scr[rs] * jnp.tile(alpha, (1, A_KV_RANK // 128)) + _dot(p, kv)

        sc = {0: scores(0)}
        pa = {}
        for i in range(n_strips):
            if i + 1 < n_strips:
                sc[i + 1] = scores(i + 1)
            pa[i] = softmax_update(i, sc.pop(i))
            if i > 0:
                values(i - 1, *pa.pop(i - 1))
        values(n_strips - 1, *pa.pop(n_strips - 1))
        return carry

    lax.fori_loop(0, n_tiles, body, 0)
    o = acc_scr[...] / jnp.tile(l_scr[...], (1, A_KV_RANK // 128))
    o_ref[0, 0] = o.astype(BF16).reshape(A_HEADS, Q_BLOCK, A_KV_RANK)


def _dsa_attn(qatt, keys, bias, tk=512, n_strips=8):
    b, nb = qatt.shape[:2]
    s = keys.shape[1]
    tk = min(tk, s)
    rows = A_HEADS * Q_BLOCK
    kern = functools.partial(_dsa_attn_kernel, tk=tk, n_strips=n_strips)
    return pl.pallas_call(
        kern,
        grid=(b, nb),
        in_specs=[pl.BlockSpec((1, 1, A_HEADS, Q_BLOCK, A_KEY), lambda bi, i: (bi, i, 0, 0, 0)),
                  pl.BlockSpec((1, s, A_KEY), lambda bi, i: (bi, 0, 0)),
                  pl.BlockSpec((1, Q_BLOCK, s), lambda bi, i: (bi, i, 0))],
        out_specs=pl.BlockSpec((1, 1, A_HEADS, Q_BLOCK, A_KV_RANK), lambda bi, i: (bi, i, 0, 0, 0)),
        out_shape=jax.ShapeDtypeStruct((b, nb, A_HEADS, Q_BLOCK, A_KV_RANK), BF16),
        scratch_shapes=[pltpu.VMEM((rows, 128), F32), pltpu.VMEM((rows, 128), F32),
                        pltpu.VMEM((rows, A_KV_RANK), F32)],
        compiler_params=_params("parallel", "parallel"),
    )(qatt, keys, bias)


def _dsa_out_kernel(ol_ref, wuv_ref, wo_ref, x_ref, g_ref, o_ref, y_scr):
    nb = ol_ref.shape[1]
    tm = nb * Q_BLOCK
    for h in range(A_HEADS):
        oh = ol_ref[0, :, h].reshape(tm, A_KV_RANK)
        y_scr[:, h * A_V:(h + 1) * A_V] = _dot(oh, wuv_ref[h]).astype(BF16)
    o_ref[0] = x_ref[0] + g_ref[0] * _dot(y_scr[...], wo_ref[...])


def _dsa_out(olat, w_uv_h, w_o, x, g, tm=512):
    b, s, d = x.shape
    nb = tm // Q_BLOCK
    return pl.pallas_call(
        _dsa_out_kernel,
        grid=(b, s // tm),
        in_specs=[pl.BlockSpec((1, nb, A_HEADS, Q_BLOCK, A_KV_RANK), lambda bi, i: (bi, i, 0, 0, 0)),
                  pl.BlockSpec(w_uv_h.shape, lambda bi, i: (0, 0, 0)),
                  pl.BlockSpec(w_o.shape, lambda bi, i: (0, 0)),
                  pl.BlockSpec((1, tm, d), lambda bi, i: (bi, i, 0)),
                  pl.BlockSpec((1, 1, d), lambda bi, i: (bi, 0, 0))],
        out_specs=pl.BlockSpec((1, tm, d), lambda bi, i: (bi, i, 0)),
        out_shape=jax.ShapeDtypeStruct((b, s, d), F32),
        scratch_shapes=[pltpu.VMEM((tm, A_HEADS * A_V), BF16)],
        compiler_params=_params("parallel", "parallel"),
    )(olat, w_uv_h, w_o, x, g)


def _rope_tables(positions):
    def angles(dim):
        inv = 1.0 / (ROPE_BASE ** (jnp.arange(0, dim, 2, dtype=F32) / dim))
        ang = positions.astype(F32)[..., None] * inv
        return jnp.cos(ang), jnp.sin(ang)

    cos_a, sin_a = angles(A_ROPE)
    assert A_ROPE == 2 * IDX_ROPE
    cos_i, sin_i = cos_a[..., ::2], sin_a[..., ::2]
    return cos_a, sin_a, cos_i, sin_i


def _dsa_mixer(x, sc, sh, g, positions, w_in, qg, kvg, w_uq, w_qidx, lng, lnb, w_uk, w_uv, w_o):
    b, s, d = x.shape
    topk = min(TOPK_MAX, s // 4)
    cos_a, sin_a, cos_i, sin_i = _rope_tables(positions)
    w_in_p = jnp.pad(w_in, ((0, 0), (0, A_IN_PAD - A_IN))).astype(BF16)
    cq, keys, kidx, widx = _dsa_in(x, sc, sh, w_in_p, qg.reshape(1, -1), kvg.reshape(1, -1),
                                   lng.reshape(1, -1), lnb.reshape(1, -1), cos_a, sin_a, cos_i, sin_i)

    rq = w_uq.shape[0]
    w_uq_h = w_uq.reshape(rq, A_HEADS, A_NOPE + A_ROPE)
    w_lat = _fold_uq_uk(jnp.transpose(w_uq_h[:, :, :A_NOPE], (1, 0, 2)), jnp.transpose(w_uk, (1, 2, 0)))
    w_lat = jnp.transpose(w_lat, (1, 0, 2)).reshape(rq, A_HEADS * A_KV_RANK)
    w_r = w_uq_h[:, :, A_NOPE:]
    w_rs = jnp.concatenate([w_r[..., A_ROPE // 2:], w_r[..., :A_ROPE // 2]], axis=-1)
    w_i = w_qidx.reshape(rq, IDX_HEADS, IDX_DIM)
    hr = IDX_ROPE // 2
    w_is = jnp.concatenate([w_i[..., hr:IDX_ROPE], w_i[..., :hr], jnp.zeros_like(w_i[..., IDX_ROPE:])], axis=-1)
    w_q = jnp.concatenate([w_lat, w_r.reshape(rq, -1).astype(BF16), w_rs.reshape(rq, -1).astype(BF16),
                           w_i.reshape(rq, -1).astype(BF16), w_is.reshape(rq, -1).astype(BF16)], axis=1)
    qatt, qidx = _dsa_q(cq, w_q, cos_a, sin_a, cos_i, sin_i)

    bias = _dsa_idx(qidx, kidx, jnp.transpose(widx, (0, 2, 1)), topk)
    olat = _dsa_attn(qatt, keys, bias)
    w_uv_h = jnp.transpose(w_uv, (1, 0, 2)).astype(BF16)
    return _dsa_out(olat, w_uv_h, w_o.astype(BF16), x, g)


def _rwkv_in_kernel(x_ref, xp_ref, sc_ref, sh_ref, mu_ref, wl1_ref, wr_ref, wk_ref, wv_ref,
                    w2_ref, a2_ref, g2_ref, vec_ref, seg_ref,
                    r_out, lw_out, k_out, v_out, kk_out, a_out, g_out, bonus_out,
                    mix_scr, lora_scr):
    i = pl.program_id(1)
    j = pl.program_id(2)

    @pl.when(j == 0)
    def _():
        sc = sc_ref[0]
        sh = sh_ref[0]
        h = _rms_mod(x_ref[0], sc, sh)
        hp = _rms_mod(xp_ref[0, 7:8, :], sc, sh)
        hp = jnp.where(i == 0, 0.0, hp)
        rows = lax.broadcasted_iota(jnp.int32, h.shape, 0)
        h_prev = jnp.where(rows == 0, hp, pltpu.roll(h, 1, axis=0))
        delta = h_prev - h
        for m in range(6):
            mix_scr[m] = (h + delta * mu_ref[m:m + 1, :]).astype(BF16)
        tw = jnp.tanh(_dot(mix_scr[1], wl1_ref[:, :LORA_PAD]))
        ta = _dot(mix_scr[4], wl1_ref[:, LORA_PAD:2 * LORA_PAD])
        tg = _sigmoid(_dot(mix_scr[5], wl1_ref[:, 2 * LORA_PAD:]))
        lora_scr[:, :LORA_PAD] = tw.astype(BF16)
        lora_scr[:, LORA_PAD:2 * LORA_PAD] = ta.astype(BF16)
        lora_scr[:, 2 * LORA_PAD:] = tg.astype(BF16)

    w0 = vec_ref[0:1, :]
    a0 = vec_ref[1:2, :]
    k_k = vec_ref[2:3, :]
    k_a = vec_ref[3:4, :]
    r_k = vec_ref[4:5, :]
    seg = seg_ref[...]
    r = _dot(mix_scr[0], wr_ref[0])
    k = _dot(mix_scr[2], wk_ref[0])
    v = _dot(mix_scr[3], wv_ref[0])
    z = -(w0 + _dot(lora_scr[:, :LORA_PAD], w2_ref[...]))
    softplus = jnp.maximum(z, 0.0) + jnp.log(1.0 + jnp.exp(-jnp.abs(z)))
    lw = -jnp.exp(-softplus - 0.5)
    a = _sigmoid(a0 + _dot(lora_scr[:, LORA_PAD:2 * LORA_PAD], a2_ref[...]))
    g = _dot(lora_scr[:, 2 * LORA_PAD:], g2_ref[...])
    kk = k * k_k
    nrm = jnp.sqrt(_segsum(kk * kk, seg))
    kk = kk / jnp.maximum(nrm, 1e-12)
    k2 = k * (1.0 + (a - 1.0) * k_a)
    bonus = _segsum(r * k2 * r_k, seg) * v
    r_out[0] = r.astype(BF16)
    lw_out[0] = lw
    k_out[0] = k2.astype(BF16)
    v_out[0] = v.astype(BF16)
    kk_out[0] = kk.astype(BF16)
    a_out[0] = a.astype(BF16)
    g_out[0] = g.astype(BF16)
    bonus_out[0] = bonus.astype(BF16)


def _rwkv_in(x, sc, sh, mu, wl1, w_r, w_k, w_v, w2, a2, g2, vec, seg, tm=512, tn=256):
    b, s, d = x.shape
    sub = 8
    vecs = pl.BlockSpec((1, 1, d), lambda bi, i, j: (bi, 0, 0))
    col = lambda rows: pl.BlockSpec((rows, tn), lambda bi, i, j: (0, j))
    tile = pl.BlockSpec((1, d, tn), lambda bi, i, j: (j, 0, 0))
    out = pl.BlockSpec((1, tm, tn), lambda bi, i, j: (bi, i, j))
    return pl.pallas_call(
        _rwkv_in_kernel,
        grid=(b, s // tm, d // tn),
        in_specs=[pl.BlockSpec((1, tm, d), lambda bi, i, j: (bi, i, 0)),
                  pl.BlockSpec((1, sub, d), lambda bi, i, j: (bi, jnp.maximum(i * (tm // sub) - 1, 0), 0)),
                  vecs, vecs,
                  pl.BlockSpec(mu.shape, lambda bi, i, j: (0, 0)),
                  pl.BlockSpec(wl1.shape, lambda bi, i, j: (0, 0)),
                  tile, tile, tile, col(LORA_PAD), col(LORA_PAD), col(R_GATE_LORA),
                  col(8), pl.BlockSpec(seg.shape, lambda bi, i, j: (0, 0))],
        out_specs=[out] * 8,
        out_shape=[jax.ShapeDtypeStruct((b, s, d), F32 if i == 1 else BF16) for i in range(8)],
        scratch_shapes=[pltpu.VMEM((6, tm, d), BF16), pltpu.VMEM((tm, 2 * LORA_PAD + R_GATE_LORA), BF16)],
        compiler_params=_params("parallel", "parallel", "arbitrary"),
    )(x, x, sc, sh, mu, wl1, _col_tiles(w_r, tn), _col_tiles(w_k, tn), _col_tiles(w_v, tn), w2, a2, g2, vec, seg)


def _rwkv_rec_kernel(r_ref, lw_ref, k_ref, v_ref, kk_ref, a_ref, o_ref, s_scr, *, n_chunks, n_heads):
    t = pl.program_id(2)

    @pl.when(t == 0)
    def _():
        s_scr[...] = jnp.zeros_like(s_scr)

    c_len = CHUNK
    n = R_HEAD
    ri = lax.broadcasted_iota(jnp.int32, (c_len, c_len), 0)
    ci = lax.broadcasted_iota(jnp.int32, (c_len, c_len), 1)
    strict = ri > ci
    incl = ri >= ci
    tri = jnp.where(incl, 1.0, 0.0).astype(BF16)
    items = [(c, h) for c in range(n_chunks) for h in range(n_heads)]

    pre = []
    for c in range(n_chunks):
        sl = slice(c * c_len, (c + 1) * c_len)
        lw = lw_ref[0, sl, :]
        k = k_ref[0, sl, :].astype(F32)
        kk = kk_ref[0, sl, :].astype(F32)
        kb = kk * a_ref[0, sl, :].astype(F32)
        cum = _cumsum_rows(tri, lw)
        cum_c = cum[c_len - 1:c_len, :]
        p_inv = jnp.exp(-cum)
        p_hat = jnp.exp(cum_c - cum)
        pre.append(dict(
            r_t=r_ref[0, sl, :].astype(F32) * jnp.exp(cum), a_t=-kk * jnp.exp(cum - lw), b_t=kb * p_inv,
            k_t=k * p_inv, b_h=kb * p_hat, k_h=k * p_hat, p_c=jnp.exp(cum_c), v=v_ref[0, sl, :].astype(F32)))

    def head(c, h, name):
        return pre[c][name][:, h * n:(h + 1) * n]

    amat = {}
    for c, h in items:
        ar = jnp.concatenate([head(c, h, "a_t"), head(c, h, "r_t")], axis=0)
        bk = jnp.concatenate([head(c, h, "b_t"), head(c, h, "k_t")], axis=0)
        amat[c, h] = _mm(ar, bk, NT_DIMS, PASSES_PAIR)
    l_ab, m_rb, lv = {}, {}, {}
    for c, h in items:
        am = amat[c, h]
        l_ab[c, h] = jnp.where(strict, am[:c_len, :c_len], 0.0)
        m_rb[c, h] = jnp.where(incl, am[c_len:, :c_len], 0.0)
        lm = jnp.concatenate([jnp.where(strict, am[:c_len, c_len:], 0.0),
                              jnp.where(incl, am[c_len:, c_len:], 0.0)], axis=0)
        lv[c, h] = _mm(lm, head(c, h, "v"), NN_DIMS, PASSES_PAIR)
    z = {}
    for c, h in items:
        z[c, h] = jnp.concatenate([head(c, h, "a_t"), lv[c, h][:c_len], l_ab[c, h]], axis=1)
    for it in range(6):
        for c, h in items:
            zz = z[c, h]
            pw = zz[:, 2 * n:]
            if it < 5:
                x = _mm(pw, zz, NN_DIMS, PASSES_SOLVE)
                z[c, h] = jnp.concatenate([zz[:, :2 * n] + x[:, :2 * n], x[:, 2 * n:]], axis=1)
            else:
                z[c, h] = zz[:, :2 * n] + _mm(pw, zz[:, :2 * n], NN_DIMS, PASSES_SOLVE)
    state = [s_scr[h] for h in range(n_heads)]
    for c in range(n_chunks):
        u = [_mm(z[c, h][:, :n], state[h], NT_DIMS, PASSES_STATE) + z[c, h][:, n:] for h in range(n_heads)]
        o = [_mm(head(c, h, "r_t"), state[h], NT_DIMS, PASSES_STATE) + _mm(m_rb[c, h], u[h], NN_DIMS, PASSES_STATE)
             + lv[c, h][c_len:] for h in range(n_heads)]
        state = [state[h] * head(c, h, "p_c")
                 + _mm(u[h], head(c, h, "b_h"), TN_DIMS, PASSES_STATE)
                 + _mm(head(c, h, "v"), head(c, h, "k_h"), TN_DIMS, PASSES_STATE) for h in range(n_heads)]
        o_ref[0, c * c_len:(c + 1) * c_len, :] = jnp.concatenate(o, axis=1)
    for h in range(n_heads):
        s_scr[h] = state[h]


def _mm(a, b, dims=NN_DIMS, passes=1):
    if passes == 1:
        return _dot(a.astype(BF16), b.astype(BF16), dims)
    return _dot3(a, b, dims)


def _cumsum_rows(tri, x):
    hi, lo = _split(x)
    lo2 = (x - hi.astype(F32) - lo.astype(F32)).astype(BF16)
    return _dot(tri, hi) + _dot(tri, lo) + _dot(tri, lo2)


def _rwkv_rec(r, lw, k, v, kk, a, tb=256, lanes=512):
    b, s, d = r.shape
    tb = min(tb, s)
    n_heads = lanes // R_HEAD
    kern = functools.partial(_rwkv_rec_kernel, n_chunks=tb // CHUNK, n_heads=n_heads)
    blk = pl.BlockSpec((1, tb, lanes), lambda bi, hi, ti: (bi, ti, hi))
    return pl.pallas_call(
        kern,
        grid=(b, d // lanes, s // tb),
        in_specs=[blk] * 6,
        out_specs=blk,
        out_shape=jax.ShapeDtypeStruct((b, s, d), F32),
        scratch_shapes=[pltpu.VMEM((n_heads, R_HEAD, R_HEAD), F32)],
        compiler_params=_params("parallel", "parallel", "arbitrary"),
    )(r, lw, k, v, kk, a)


def _rwkv_out_kernel(o_ref, g_ref, bonus_ref, x_ref, gate_ref, gng_ref, gnb_ref, seg_ref, wo_ref,
                     out_ref, y_scr):
    d = x_ref.shape[2]
    tn = seg_ref.shape[0]
    seg = seg_ref[...]
    inv_n = 1.0 / R_HEAD
    for jt in range(d // tn):
        sl = slice(jt * tn, (jt + 1) * tn)
        o = o_ref[0, :, sl]
        mu = _segsum(o, seg) * inv_n
        dv = o - mu
        var = _segsum(dv * dv, seg) * inv_n
        ln = dv * lax.rsqrt(var + R_GN_EPS)
        y = ((ln * gng_ref[:, sl] + gnb_ref[:, sl] + bonus_ref[0, :, sl].astype(F32))
             * g_ref[0, :, sl].astype(F32))
        y_scr[:, sl] = y.astype(BF16)
    out_ref[0] = x_ref[0] + gate_ref[0] * _dot(y_scr[...], wo_ref[...])


def _rwkv_out(o, g, bonus, x, gate, gng, gnb, seg, w_o, tm=512):
    b, s, d = x.shape
    tok = pl.BlockSpec((1, tm, d), lambda bi, i: (bi, i, 0))
    full = lambda a: pl.BlockSpec(a.shape, lambda bi, i: (0,) * a.ndim)
    return pl.pallas_call(
        _rwkv_out_kernel,
        grid=(b, s // tm),
        in_specs=[tok, tok, tok, tok, pl.BlockSpec((1, 1, d), lambda bi, i: (bi, 0, 0)),
                  full(gng), full(gnb), full(seg), full(w_o)],
        out_specs=tok,
        out_shape=jax.ShapeDtypeStruct((b, s, d), F32),
        scratch_shapes=[pltpu.VMEM((tm, d), BF16)],
        compiler_params=_params("parallel", "parallel"),
    )(o, g, bonus, x, gate, gng, gnb, seg, w_o)


def _rwkv_mixer(x, sc, sh, gate, mu, w_r, w_k, w_v, w_o, w0, w_w1, w_w2, a0, w_a1, w_a2, w_g1, w_g2,
                k_k, k_a, r_k, gn_g, gn_b):
    d = x.shape[2]
    tn = 256
    lora = w_w1.shape[1]
    padc = lambda w: jnp.pad(w, ((0, 0), (0, LORA_PAD - lora)))
    padr = lambda w: jnp.pad(w, ((0, LORA_PAD - lora), (0, 0)))
    wl1 = jnp.concatenate([padc(w_w1), padc(w_a1), w_g1], axis=1).astype(BF16)
    vec = jnp.stack([w0, a0, k_k, k_a, r_k.reshape(-1), w0 * 0, w0 * 0, w0 * 0])
    idx = jnp.arange(tn) // R_HEAD
    seg = (idx[:, None] == idx[None, :]).astype(BF16)
    r, lw, k2, v, kk, a, g, bonus = _rwkv_in(
        x, sc, sh, mu, wl1, w_r.astype(BF16), w_k.astype(BF16), w_v.astype(BF16),
        padr(w_w2).astype(BF16), padr(w_a2).astype(BF16), w_g2.astype(BF16), vec, seg, tn=tn)
    o = _rwkv_rec(r, lw, k2, v, kk, a)
    return _rwkv_out(o, g, bonus, x, gate, gn_g.reshape(1, d), gn_b.reshape(1, d), seg, w_o.astype(BF16))


def kernel(x, c, positions, ada_w, ada_b, mlp_w1, mlp_w2, final_g, a_w_in, a_q_norm_g, a_kv_norm_g, a_w_uq,
           a_w_qidx, a_kidx_ln_g, a_kidx_ln_b, a_w_uk, a_w_uv, a_w_o, b_mu, b_w_r, b_w_k, b_w_v, b_w_o, b_w0,
           b_w_w1, b_w_w2, b_a0, b_w_a1, b_w_a2, b_w_g1, b_w_g2, b_k_k, b_k_a, b_r_k, b_gn_g, b_gn_b):
    depth = ada_w.shape[0]
    d = x.shape[2]
    mod = _ada_mod(c, ada_w, ada_b)
    for i in range(depth):
        sh1, sc1, g1, sh2, sc2, g2 = [mod[i, :, None, m * d:(m + 1) * d] for m in range(6)]
        j = i // 2
        if i % 2 == 0:
            x = _dsa_mixer(x, sc1, sh1, g1, positions, a_w_in[j], a_q_norm_g[j], a_kv_norm_g[j], a_w_uq[j],
                           a_w_qidx[j], a_kidx_ln_g[j], a_kidx_ln_b[j], a_w_uk[j], a_w_uv[j], a_w_o[j])
        else:
            x = _rwkv_mixer(x, sc1, sh1, g1, b_mu[j], b_w_r[j], b_w_k[j], b_w_v[j], b_w_o[j], b_w0[j],
                            b_w_w1[j], b_w_w2[j], b_a0[j], b_w_a1[j], b_w_a2[j], b_w_g1[j], b_w_g2[j],
                            b_k_k[j], b_k_a[j], b_r_k[j], b_gn_g[j], b_gn_b[j])
        x = _mlp(x, sc2, sh2, g2, mlp_w1[i].astype(BF16), mlp_w2[i].astype(BF16), final_g,
                 final_norm=(i == depth - 1))
    return x
```

```python
import functools

import jax
import jax.numpy as jnp
from jax import lax
from jax.experimental import pallas as pl
from jax.experimental.pallas import tpu as pltpu

F32 = jnp.float32
BF16 = jnp.bfloat16

D_MODEL = 2048
DEPTH = 2
CHUNK = 64
EPS = 1e-6
A_HEADS = 16
A_NOPE = 128
A_ROPE = 64
A_V = 128
A_Q_RANK = 512
A_KV_RANK = 256
A_KEY = A_KV_RANK + A_ROPE
IDX_HEADS = 16
IDX_DIM = 64
IDX_ROPE = 32
TOPK_MAX = 256
Q_BLOCK = 128
ROPE_BASE = 10000.0
A_SCALE = (A_NOPE + A_ROPE) ** -0.5
A_IN = A_Q_RANK + A_KV_RANK + A_ROPE + IDX_DIM + IDX_HEADS
A_IN_PAD = 1024
R_HEAD = 64
R_GN_EPS = R_HEAD * 1e-5
LORA_PAD = 128
R_GATE_LORA = 256

VMEM_LIMIT_BYTES = 56 * 1024 * 1024
MASK_VALUE = -1e30

NT_DIMS = (((1,), (1,)), ((), ()))
TN_DIMS = (((0,), (0,)), ((), ()))
NN_DIMS = (((1,), (0,)), ((), ()))

PASSES_PAIR = 1
PASSES_SOLVE = 1
PASSES_STATE = 1


def _params(*sem):
    return pltpu.CompilerParams(dimension_semantics=sem, vmem_limit_bytes=VMEM_LIMIT_BYTES)


def _dot(a, b, dims=NN_DIMS):
    return lax.dot_general(a, b, dims, preferred_element_type=F32)


def _split(x):
    hi = x.astype(BF16)
    lo = (x - hi.astype(F32)).astype(BF16)
    return hi, lo


def _dot3(a, b, dims=NN_DIMS):
    ah, al = _split(a)
    bh, bl = _split(b)
    return _dot(ah, bh, dims) + _dot(ah, bl, dims) + _dot(al, bh, dims)


def _segsum(x, seg):
    return _dot(x.astype(BF16), seg)


def _cast_kernel(x_ref, o_ref):
    o_ref[...] = x_ref[...].astype(BF16)


def _to_bf16(w, block_bytes=8 * 1024 * 1024):
    l, r, c = w.shape
    tr = min(r, max(8, block_bytes // (4 * c)))
    assert r % tr == 0
    blk = pl.BlockSpec((1, tr, c), lambda li, i: (li, i, 0))
    return pl.pallas_call(
        _cast_kernel, grid=(l, r // tr), in_specs=[blk], out_specs=blk,
        out_shape=jax.ShapeDtypeStruct(w.shape, BF16),
        compiler_params=_params("parallel", "parallel"),
    )(w)


def _rms(x):
    return x * lax.rsqrt(jnp.mean(x * x, axis=-1, keepdims=True) + EPS)


def _rms_mod(x, sc, sh):
    return _rms(x) * (1.0 + sc) + sh


def _sigmoid(x):
    return 1.0 / (1.0 + jnp.exp(-x))


def _rope_half(x, c, s):
    d = x.shape[-1] // 2
    x1 = x[:, :d]
    x2 = x[:, d:]
    return jnp.concatenate([x1 * c - x2 * s, x2 * c + x1 * s], axis=-1)


def _mod_kernel(c_ref, w_ref, b_ref, o_ref):
    c = c_ref[...]
    a = (c * _sigmoid(c)).astype(BF16)
    o_ref[0] = _dot(a, w_ref[0].astype(BF16)) + b_ref[0]


def _ada_mod(c, ada_w, ada_b):
    depth, d, n = ada_w.shape
    b = c.shape[0]
    tn = 1024
    return pl.pallas_call(
        _mod_kernel,
        grid=(depth, n // tn),
        in_specs=[
            pl.BlockSpec((b, d), lambda l, j: (0, 0)),
            pl.BlockSpec((1, d, tn), lambda l, j: (l, 0, j)),
            pl.BlockSpec((1, 1, tn), lambda l, j: (l, 0, j)),
        ],
        out_specs=pl.BlockSpec((1, b, tn), lambda l, j: (l, 0, j)),
        out_shape=jax.ShapeDtypeStruct((depth, b, n), F32),
        compiler_params=_params("arbitrary", "arbitrary"),
    )(c, ada_w, ada_b.reshape(depth, 1, n))


def _mlp_kernel(x_ref, xn_ref, sc_ref, sh_ref, scn_ref, shn_ref, g_ref, w1_ref, w2_ref, fg_ref, o_ref,
                h_scr, acc_scr, *, final_norm):
    f = pl.program_id(2)
    last = pl.num_programs(2) - 1

    @pl.when((pl.program_id(0) == 0) & (pl.program_id(1) == 0) & (f == 0))
    def _():
        h_scr[...] = _rms_mod(x_ref[0], sc_ref[0], sh_ref[0]).astype(BF16)
        acc_scr[...] = jnp.zeros_like(acc_scr)

    def hidden():
        a = jnp.maximum(_dot(h_scr[...], w1_ref[0]), 0.0)
        return _dot((a * a).astype(BF16), w2_ref[0])

    @pl.when(f < last)
    def _():
        acc_scr[...] += hidden()

    @pl.when(f == last)
    def _():
        y = x_ref[0] + g_ref[0] * (acc_scr[...] + hidden())
        if final_norm:
            y = _rms(y) * fg_ref[...]
        o_ref[0] = y
        h_scr[...] = _rms_mod(xn_ref[0], scn_ref[0], shn_ref[0]).astype(BF16)
        acc_scr[...] = jnp.zeros_like(acc_scr)


def _mlp(x, sc, sh, g, w1, w2, layer, final_g, final_norm, tm=512, tf=1024):
    b, s, d = x.shape
    dff = w1.shape[2]
    n_i = s // tm
    kern = functools.partial(_mlp_kernel, final_norm=final_norm)

    def next_block(bi, i):
        return jnp.minimum(bi * n_i + i + 1, b * n_i - 1)

    vec = pl.BlockSpec((1, 1, d), lambda bi, i, f: (bi, 0, 0))
    vec_next = pl.BlockSpec((1, 1, d), lambda bi, i, f: (next_block(bi, i) // n_i, 0, 0))
    return pl.pallas_call(
        kern,
        grid=(b, n_i, dff // tf),
        in_specs=[
            pl.BlockSpec((1, tm, d), lambda bi, i, f: (bi, i, 0)),
            pl.BlockSpec((1, tm, d), lambda bi, i, f: (next_block(bi, i) // n_i, next_block(bi, i) % n_i, 0)),
            vec, vec, vec_next, vec_next, vec,
            pl.BlockSpec((1, d, tf), lambda bi, i, f: (layer, 0, f)),
            pl.BlockSpec((1, tf, d), lambda bi, i, f: (layer, f, 0)),
            pl.BlockSpec((1, d), lambda bi, i, f: (0, 0)),
        ],
        out_specs=pl.BlockSpec((1, tm, d), lambda bi, i, f: (bi, i, 0)),
        out_shape=jax.ShapeDtypeStruct((b, s, d), F32),
        scratch_shapes=[pltpu.VMEM((tm, d), BF16), pltpu.VMEM((tm, d), F32)],
        compiler_params=_params("arbitrary", "arbitrary", "arbitrary"),
    )(x, x, sc, sh, sc, sh, g, w1, w2, final_g.reshape(1, d))


def _dsa_in_kernel(x_ref, sc_ref, sh_ref, w_ref, qg_ref, kvg_ref, lng_ref, lnb_ref,
                   ca_ref, sa_ref, ci_ref, si_ref, cq_ref, keys_ref, kidx_ref, widx_ref):
    o_kv = A_Q_RANK
    o_kr = o_kv + A_KV_RANK
    o_ki = o_kr + A_ROPE
    o_wi = o_ki + IDX_DIM
    tm = x_ref.shape[1]
    halves = [slice(0, tm // 2), slice(tm // 2, tm)]
    hs = [_rms_mod(x_ref[0, rs], sc_ref[0], sh_ref[0]).astype(BF16) for rs in halves]
    projs = [_dot(h, w_ref[...]) for h in hs]
    for rs, proj in zip(halves, projs):
        cq_ref[0, rs] = (_rms(proj[:, :o_kv]) * qg_ref[...]).astype(BF16)
        keys_ref[0, rs, :A_KV_RANK] = (_rms(proj[:, o_kv:o_kr]) * kvg_ref[...]).astype(BF16)
        keys_ref[0, rs, A_KV_RANK:] = _rope_half(proj[:, o_kr:o_ki], ca_ref[0, rs], sa_ref[0, rs]).astype(BF16)
        ki = proj[:, o_ki:o_wi]
        mu = jnp.mean(ki, axis=-1, keepdims=True)
        kc = ki - mu
        var = jnp.mean(kc * kc, axis=-1, keepdims=True)
        ki = kc * lax.rsqrt(var + EPS) * lng_ref[...] + lnb_ref[...]
        ki = jnp.concatenate([_rope_half(ki[:, :IDX_ROPE], ci_ref[0, rs], si_ref[0, rs]), ki[:, IDX_ROPE:]],
                             axis=-1)
        kidx_ref[0, rs] = ki.astype(BF16)
        widx_ref[0, rs] = proj[:, o_wi:o_wi + IDX_HEADS] * (IDX_HEADS ** -0.5 * IDX_DIM ** -0.5)


def _dsa_in(x, sc, sh, w_in_p, qg, kvg, lng, lnb, cos_a, sin_a, cos_i, sin_i, tm=512):
    b, s, d = x.shape
    vec = pl.BlockSpec((1, 1, d), lambda bi, i: (bi, 0, 0))

    def full(a):
        return pl.BlockSpec(a.shape, lambda bi, i: (0,) * a.ndim)

    def tok(w):
        return pl.BlockSpec((1, tm, w), lambda bi, i: (bi, i, 0))

    return pl.pallas_call(
        _dsa_in_kernel,
        grid=(b, s // tm),
        in_specs=[tok(d), vec, vec, full(w_in_p), full(qg), full(kvg), full(lng), full(lnb),
                  tok(A_ROPE // 2), tok(A_ROPE // 2), tok(IDX_ROPE // 2), tok(IDX_ROPE // 2)],
        out_specs=[tok(A_Q_RANK), tok(A_KEY), tok(IDX_DIM), tok(IDX_HEADS)],
        out_shape=[jax.ShapeDtypeStruct((b, s, A_Q_RANK), BF16),
                   jax.ShapeDtypeStruct((b, s, A_KEY), BF16),
                   jax.ShapeDtypeStruct((b, s, IDX_DIM), BF16),
                   jax.ShapeDtypeStruct((b, s, IDX_HEADS), F32)],
        compiler_params=_params("parallel", "parallel"),
    )(x, sc, sh, w_in_p, qg, kvg, lng, lnb, cos_a, sin_a, cos_i, sin_i)


def _fold_kernel(a_ref, b_ref, o_ref):
    o_ref[0] = _dot3(a_ref[0], b_ref[0]).astype(BF16)


def _fold_uq_uk(w_uq_nope, w_uk_t):
    h, rq, dn = w_uq_nope.shape
    rkv = w_uk_t.shape[2]
    return pl.pallas_call(
        _fold_kernel,
        grid=(h,),
        in_specs=[pl.BlockSpec((1, rq, dn), lambda i: (i, 0, 0)),
                  pl.BlockSpec((1, dn, rkv), lambda i: (i, 0, 0))],
        out_specs=pl.BlockSpec((1, rq, rkv), lambda i: (i, 0, 0)),
        out_shape=jax.ShapeDtypeStruct((h, rq, rkv), BF16),
        compiler_params=_params("arbitrary"),
    )(w_uq_nope, w_uk_t)


def _dsa_q_kernel(cq_ref, w_ref, ca_ref, sa_ref, ci_ref, si_ref, qatt_ref, qidx_ref):
    cq = cq_ref[0]
    tm = cq.shape[0]
    ca, sa, ci, si = ca_ref[0], sa_ref[0], ci_ref[0], si_ref[0]
    one = jnp.ones((tm, IDX_DIM - IDX_ROPE), F32)
    cq_t = jnp.concatenate([ca, ca, ca, ca], axis=-1)
    sq_t = jnp.concatenate([-sa, sa, -sa, sa], axis=-1)
    ci_t = jnp.concatenate([ci, ci, one, ci, ci, one], axis=-1)
    si_t = jnp.concatenate([-si, si, 0.0 * one, -si, si, 0.0 * one], axis=-1)
    nb = tm // Q_BLOCK
    n_lat = A_HEADS * A_KV_RANK
    n_rope = A_HEADS * A_ROPE
    n_idx = IDX_HEADS * IDX_DIM
    for h in range(A_HEADS):
        lat = _dot(cq, w_ref[:, h * A_KV_RANK:(h + 1) * A_KV_RANK]) * A_SCALE
        qatt_ref[0, :, h, :, :A_KV_RANK] = lat.astype(BF16).reshape(nb, Q_BLOCK, A_KV_RANK)
    o = n_lat
    r = _dot(cq, w_ref[:, o:o + n_rope])
    rs = _dot(cq, w_ref[:, o + n_rope:o + 2 * n_rope])
    rep = n_rope // cq_t.shape[1]
    qr = ((r * jnp.tile(cq_t, (1, rep)) + rs * jnp.tile(sq_t, (1, rep))) * A_SCALE).astype(BF16)
    for h in range(A_HEADS):
        qatt_ref[0, :, h, :, A_KV_RANK:] = qr[:, h * A_ROPE:(h + 1) * A_ROPE].reshape(nb, Q_BLOCK, A_ROPE)
    o = n_lat + 2 * n_rope
    qi = _dot(cq, w_ref[:, o:o + n_idx])
    qis = _dot(cq, w_ref[:, o + n_idx:o + 2 * n_idx])
    rep = n_idx // ci_t.shape[1]
    qi = (qi * jnp.tile(ci_t, (1, rep)) + qis * jnp.tile(si_t, (1, rep))).astype(BF16)
    for h in range(IDX_HEADS):
        qidx_ref[0, :, h] = qi[:, h * IDX_DIM:(h + 1) * IDX_DIM].reshape(nb, Q_BLOCK, IDX_DIM)


def _dsa_q(cq, w_q, cos_a, sin_a, cos_i, sin_i, tm=512):
    b, s, rq = cq.shape
    nb = tm // Q_BLOCK

    def tok(w):
        return pl.BlockSpec((1, tm, w), lambda bi, i: (bi, i, 0))

    return pl.pallas_call(
        _dsa_q_kernel,
        grid=(b, s // tm),
        in_specs=[tok(rq), pl.BlockSpec(w_q.shape, lambda bi, i: (0, 0)),
                  tok(A_ROPE // 2), tok(A_ROPE // 2), tok(IDX_ROPE // 2), tok(IDX_ROPE // 2)],
        out_specs=[pl.BlockSpec((1, nb, A_HEADS, Q_BLOCK, A_KEY), lambda bi, i: (bi, i, 0, 0, 0)),
                   pl.BlockSpec((1, nb, IDX_HEADS, Q_BLOCK, IDX_DIM), lambda bi, i: (bi, i, 0, 0, 0))],
        out_shape=[jax.ShapeDtypeStruct((b, s // Q_BLOCK, A_HEADS, Q_BLOCK, A_KEY), BF16),
                   jax.ShapeDtypeStruct((b, s // Q_BLOCK, IDX_HEADS, Q_BLOCK, IDX_DIM), BF16)],
        compiler_params=_params("parallel", "parallel"),
    )(cq, w_q, cos_a, sin_a, cos_i, sin_i)


KEY_NEG_INF = -2139095041


def _key_to_float(key):
    return pltpu.bitcast(key ^ ((key >> 31) & jnp.int32(0x7FFFFFFF)), F32)


def _dsa_idx_kernel(qidx_ref, kidx_ref, widx_ref, bias_ref, score_scr, last_scr, *, topk, tk):
    qb = pl.program_id(1)
    s = kidx_ref.shape[1]
    n_tiles = ((qb + 1) * Q_BLOCK + tk - 1) // tk
    q_all = qidx_ref[0, 0].reshape(IDX_HEADS * Q_BLOCK, IDX_DIM)
    w = widx_ref[0]
    q_chunk = (qb * Q_BLOCK + lax.broadcasted_iota(jnp.int32, (tk, Q_BLOCK), 1)) // CHUNK
    key0 = lax.broadcasted_iota(jnp.int32, (tk, Q_BLOCK), 0)
    sub = min(tk, 128)

    def tile_off(t):
        return pl.multiple_of(t * tk, tk)

    def allowed_tile(off):
        return ((key0 + off) // CHUNK) <= q_chunk

    def score_tile(t, carry):
        off = tile_off(t)
        allowed = allowed_tile(off)
        for j in range(tk // sub):
            rows = slice(j * sub, (j + 1) * sub)
            rel = _dot(kidx_ref[0, pl.ds(off + j * sub, sub), :], q_all, NT_DIMS)
            sc = jnp.zeros((sub, Q_BLOCK), F32)
            for h in range(IDX_HEADS):
                sc = sc + jnp.maximum(rel[:, h * Q_BLOCK:(h + 1) * Q_BLOCK], 0.0) * w[h:h + 1, :]
            score_scr[pl.ds(off + j * sub, sub), :] = jnp.where(allowed[rows], sc, -jnp.inf)
        return carry

    lax.fori_loop(0, n_tiles, score_tile, 0)

    cnt_rows = 64

    def count(pred):
        def body(t, acc):
            off = tile_off(t)
            hit = jnp.where(pred(score_scr[pl.ds(off, tk), :], off), 1.0, 0.0)
            for j in range(tk // cnt_rows):
                acc = acc + hit[j * cnt_rows:(j + 1) * cnt_rows]
            return acc

        acc = lax.fori_loop(0, n_tiles, body, jnp.zeros((cnt_rows, Q_BLOCK), F32))
        return jnp.sum(acc, axis=0, keepdims=True)

    kf = jnp.float32(topk)
    n_visited = (n_tiles * tk).astype(F32)

    def count_key(cand):
        cand_f = _key_to_float(cand)
        return jnp.where(cand < KEY_NEG_INF, n_visited, count(lambda sc, off: sc >= cand_f))

    thr = jnp.where(count_key(jnp.zeros((1, Q_BLOCK), jnp.int32)) >= kf, jnp.int32(0), jnp.int32(-2 ** 31))

    def bit_step(i, thr):
        cand = thr | lax.shift_left(jnp.int32(1), jnp.int32(30) - i)
        return jnp.where(count_key(cand) >= kf, cand, thr)

    thr = lax.fori_loop(0, 31, bit_step, thr)
    thr_f = _key_to_float(thr)
    take_all = thr < KEY_NEG_INF

    last_scr[...] = jnp.full((1, Q_BLOCK), s - 1, jnp.int32)
    n_ge = jnp.where(take_all, kf, count(lambda sc, off: sc >= thr_f))

    @pl.when(jnp.max(n_ge) > kf)
    def _():
        need = kf - count(lambda sc, off: sc > thr_f)

        def idx_step(i, last):
            cand = last | lax.shift_left(jnp.int32(1), jnp.int32((s - 1).bit_length() - 1) - i)
            below = count(lambda sc, off: (sc == thr_f) & (key0 + off < cand))
            return jnp.where(below < need, cand, last)

        last = lax.fori_loop(0, (s - 1).bit_length(), idx_step, jnp.zeros((1, Q_BLOCK), jnp.int32))
        last_scr[...] = jnp.where(n_ge > kf, last, s - 1)

    last = last_scr[...]

    def write_tile(t, carry):
        off = tile_off(t)
        sc = score_scr[pl.ds(off, tk), :]
        sel = ((sc > thr_f) | ((sc == thr_f) & (key0 + off <= last)) | take_all) & allowed_tile(off)
        bias_ref[0, :, pl.ds(off, tk)] = jnp.where(sel, 0.0, MASK_VALUE).T.astype(BF16)
        return carry

    lax.fori_loop(0, n_tiles, write_tile, 0)

    def mask_tile(t, carry):
        bias_ref[0, :, pl.ds(tile_off(t), tk)] = jnp.full((Q_BLOCK, tk), MASK_VALUE, BF16)
        return carry

    lax.fori_loop(n_tiles, s // tk, mask_tile, 0)


def _dsa_idx(qidx, kidx, widx_t, topk, tk=256):
    b, nb = qidx.shape[:2]
    s = kidx.shape[1]
    tk = min(tk, s)
    kern = functools.partial(_dsa_idx_kernel, topk=topk, tk=tk)
    return pl.pallas_call(
        kern,
        grid=(b, nb),
        in_specs=[pl.BlockSpec((1, 1, IDX_HEADS, Q_BLOCK, IDX_DIM), lambda bi, i: (bi, i, 0, 0, 0)),
                  pl.BlockSpec((1, s, IDX_DIM), lambda bi, i: (bi, 0, 0)),
                  pl.BlockSpec((1, IDX_HEADS, Q_BLOCK), lambda bi, i: (bi, 0, i))],
        out_specs=pl.BlockSpec((1, Q_BLOCK, s), lambda bi, i: (bi, i, 0)),
        out_shape=jax.ShapeDtypeStruct((b, s, s), BF16),
        scratch_shapes=[pltpu.VMEM((s, Q_BLOCK), F32), pltpu.VMEM((1, Q_BLOCK), jnp.int32)],
        compiler_params=_params("parallel", "parallel"),
    )(qidx, kidx, widx_t)


def _dsa_attn_kernel(q_ref, keys_ref, bias_ref, o_ref, m_scr, l_scr, acc_scr, *, tk, n_strips):
    qb = pl.program_id(1)
    hs = A_HEADS // n_strips
    rows = hs * Q_BLOCK
    m_scr[...] = jnp.full_like(m_scr, -jnp.inf)
    l_scr[...] = jnp.zeros_like(l_scr)
    acc_scr[...] = jnp.zeros_like(acc_scr)
    n_tiles = ((qb + 1) * Q_BLOCK + tk - 1) // tk

    def body(t, carry):
        off = pl.multiple_of(t * tk, tk)
        kt = keys_ref[0, pl.ds(off, tk), :]
        kv = kt[:, :A_KV_RANK]
        bias = bias_ref[0, :, pl.ds(off, tk)].astype(F32)

        def scores(i):
            q = q_ref[0, 0, i * hs:(i + 1) * hs].reshape(rows, A_KEY)
            return _dot(q, kt, NT_DIMS)

        def softmax_update(i, sc):
            rs = slice(i * rows, (i + 1) * rows)
            sc = (sc.reshape(hs, Q_BLOCK, tk) + bias[None]).reshape(rows, tk)
            m_prev = m_scr[rs]
            m_new = jnp.maximum(m_prev, jnp.max(sc, axis=1, keepdims=True))
            alpha = jnp.exp(m_prev - m_new)
            p = jnp.exp(sc - jnp.tile(m_new, (1, tk // 128)))
            l_scr[rs] = alpha * l_scr[rs] + jnp.sum(p, axis=1, keepdims=True)
            m_scr[rs] = m_new
            return p.astype(BF16), alpha

        def values(i, p, alpha):
            rs = slice(i * rows, (i + 1) * rows)
            acc_scr[rs] = acc_scr[rs] * jnp.tile(alpha, (1, A_KV_RANK // 128)) + _dot(p, kv)

        sc = {0: scores(0)}
        pa = {}
        for i in range(n_strips):
            if i + 1 < n_strips:
                sc[i + 1] = scores(i + 1)
            pa[i] = softmax_update(i, sc.pop(i))
            if i > 0:
                values(i - 1, *pa.pop(i - 1))
        values(n_strips - 1, *pa.pop(n_strips - 1))
        return carry

    lax.fori_loop(0, n_tiles, body, 0)
    o = acc_scr[...] / jnp.tile(l_scr[...], (1, A_KV_RANK // 128))
    o_ref[0, 0] = o.astype(BF16).reshape(A_HEADS, Q_BLOCK, A_KV_RANK)


def _dsa_attn(qatt, keys, bias, tk=512, n_strips=8):
    b, nb = qatt.shape[:2]
    s = keys.shape[1]
    tk = min(tk, s)
    rows = A_HEADS * Q_BLOCK
    kern = functools.partial(_dsa_attn_kernel, tk=tk, n_strips=n_strips)
    return pl.pallas_call(
        kern,
        grid=(b, nb),
        in_specs=[pl.BlockSpec((1, 1, A_HEADS, Q_BLOCK, A_KEY), lambda bi, i: (bi, i, 0, 0, 0)),
                  pl.BlockSpec((1, s, A_KEY), lambda bi, i: (bi, 0, 0)),
                  pl.BlockSpec((1, Q_BLOCK, s), lambda bi, i: (bi, i, 0))],
        out_specs=pl.BlockSpec((1, 1, A_HEADS, Q_BLOCK, A_KV_RANK), lambda bi, i: (bi, i, 0, 0, 0)),
        out_shape=jax.ShapeDtypeStruct((b, nb, A_HEADS, Q_BLOCK, A_KV_RANK), BF16),
        scratch_shapes=[pltpu.VMEM((rows, 128), F32), pltpu.VMEM((rows, 128), F32),
                        pltpu.VMEM((rows, A_KV_RANK), F32)],
        compiler_params=_params("parallel", "parallel"),
    )(qatt, keys, bias)


def _dsa_out_kernel(ol_ref, wuv_ref, wo_ref, x_ref, g_ref, o_ref, y_scr):
    nb = ol_ref.shape[1]
    tm = nb * Q_BLOCK
    for h in range(A_HEADS):
        oh = ol_ref[0, :, h].reshape(tm, A_KV_RANK)
        y_scr[:, h * A_V:(h + 1) * A_V] = _dot(oh, wuv_ref[h]).astype(BF16)
    o_ref[0] = x_ref[0] + g_ref[0] * _dot(y_scr[...], wo_ref[...])


def _dsa_out(olat, w_uv_h, w_o, x, g, tm=512):
    b, s, d = x.shape
    nb = tm // Q_BLOCK
    return pl.pallas_call(
        _dsa_out_kernel,
        grid=(b, s // tm),
        in_specs=[pl.BlockSpec((1, nb, A_HEADS, Q_BLOCK, A_KV_RANK), lambda bi, i: (bi, i, 0, 0, 0)),
                  pl.BlockSpec(w_uv_h.shape, lambda bi, i: (0, 0, 0)),
                  pl.BlockSpec(w_o.shape, lambda bi, i: (0, 0)),
                  pl.BlockSpec((1, tm, d), lambda bi, i: (bi, i, 0)),
                  pl.BlockSpec((1, 1, d), lambda bi, i: (bi, 0, 0))],
        out_specs=pl.BlockSpec((1, tm, d), lambda bi, i: (bi, i, 0)),
        out_shape=jax.ShapeDtypeStruct((b, s, d), F32),
        scratch_shapes=[pltpu.VMEM((tm, A_HEADS * A_V), BF16)],
        compiler_params=_params("parallel", "parallel"),
    )(olat, w_uv_h, w_o, x, g)


def _rope_tables(positions):
    def angles(dim):
        inv = 1.0 / (ROPE_BASE ** (jnp.arange(0, dim, 2, dtype=F32) / dim))
        ang = positions.astype(F32)[..., None] * inv
        return jnp.cos(ang), jnp.sin(ang)

    cos_a, sin_a = angles(A_ROPE)
    assert A_ROPE == 2 * IDX_ROPE
    cos_i, sin_i = cos_a[..., ::2], sin_a[..., ::2]
    return cos_a, sin_a, cos_i, sin_i


def _dsa_mixer(x, sc, sh, g, positions, w_in, qg, kvg, w_uq, w_qidx, lng, lnb, w_uk, w_uv, w_o):
    b, s, d = x.shape
    topk = min(TOPK_MAX, s // 4)
    cos_a, sin_a, cos_i, sin_i = _rope_tables(positions)
    w_in_p = jnp.pad(w_in, ((0, 0), (0, A_IN_PAD - A_IN))).astype(BF16)
    cq, keys, kidx, widx = _dsa_in(x, sc, sh, w_in_p, qg.reshape(1, -1), kvg.reshape(1, -1),
                                   lng.reshape(1, -1), lnb.reshape(1, -1), cos_a, sin_a, cos_i, sin_i)

    rq = w_uq.shape[0]
    w_uq_h = w_uq.reshape(rq, A_HEADS, A_NOPE + A_ROPE)
    w_lat = _fold_uq_uk(jnp.transpose(w_uq_h[:, :, :A_NOPE], (1, 0, 2)), jnp.transpose(w_uk, (1, 2, 0)))
    w_lat = jnp.transpose(w_lat, (1, 0, 2)).reshape(rq, A_HEADS * A_KV_RANK)
    w_r = w_uq_h[:, :, A_NOPE:]
    w_rs = jnp.concatenate([w_r[..., A_ROPE // 2:], w_r[..., :A_ROPE // 2]], axis=-1)
    w_i = w_qidx.reshape(rq, IDX_HEADS, IDX_DIM)
    hr = IDX_ROPE // 2
    w_is = jnp.concatenate([w_i[..., hr:IDX_ROPE], w_i[..., :hr], jnp.zeros_like(w_i[..., IDX_ROPE:])], axis=-1)
    w_q = jnp.concatenate([w_lat, w_r.reshape(rq, -1).astype(BF16), w_rs.reshape(rq, -1).astype(BF16),
                           w_i.reshape(rq, -1).astype(BF16), w_is.reshape(rq, -1).astype(BF16)], axis=1)
    qatt, qidx = _dsa_q(cq, w_q, cos_a, sin_a, cos_i, sin_i)

    bias = _dsa_idx(qidx, kidx, jnp.transpose(widx, (0, 2, 1)), topk)
    olat = _dsa_attn(qatt, keys, bias)
    w_uv_h = jnp.transpose(w_uv, (1, 0, 2)).astype(BF16)
    return _dsa_out(olat, w_uv_h, _to_bf16(w_o[None])[0], x, g)


def _rwkv_in_kernel(x_ref, xp_ref, sc_ref, sh_ref, mu_ref, wl1_ref, wr_ref, wk_ref, wv_ref,
                    w2_ref, a2_ref, g2_ref, vec_ref, seg_ref,
                    r_out, lw_out, k_out, v_out, kk_out, a_out, g_out, bonus_out,
                    mix_scr, lora_scr):
    i = pl.program_id(1)
    j = pl.program_id(2)

    @pl.when(j == 0)
    def _():
        sc = sc_ref[0]
        sh = sh_ref[0]
        h = _rms_mod(x_ref[0], sc, sh)
        hp = _rms_mod(xp_ref[0, 7:8, :], sc, sh)
        hp = jnp.where(i == 0, 0.0, hp)
        rows = lax.broadcasted_iota(jnp.int32, h.shape, 0)
        h_prev = jnp.where(rows == 0, hp, pltpu.roll(h, 1, axis=0))
        delta = h_prev - h
        for m in range(6):
            mix_scr[m] = (h + delta * mu_ref[m:m + 1, :]).astype(BF16)
        tw = jnp.tanh(_dot(mix_scr[1], wl1_ref[:, :LORA_PAD]))
        ta = _dot(mix_scr[4], wl1_ref[:, LORA_PAD:2 * LORA_PAD])
        tg = _sigmoid(_dot(mix_scr[5], wl1_ref[:, 2 * LORA_PAD:]))
        lora_scr[:, :LORA_PAD] = tw.astype(BF16)
        lora_scr[:, LORA_PAD:2 * LORA_PAD] = ta.astype(BF16)
        lora_scr[:, 2 * LORA_PAD:] = tg.astype(BF16)

    w0 = vec_ref[0:1, :]
    a0 = vec_ref[1:2, :]
    k_k = vec_ref[2:3, :]
    k_a = vec_ref[3:4, :]
    r_k = vec_ref[4:5, :]
    seg = seg_ref[...]
    tm = mix_scr.shape[1]

    def projections(rs):
        return (_dot(mix_scr[0, rs], wr_ref[...]), _dot(mix_scr[2, rs], wk_ref[...]),
                _dot(mix_scr[3, rs], wv_ref[...]), _dot(lora_scr[rs, :LORA_PAD], w2_ref[...]),
                _dot(lora_scr[rs, LORA_PAD:2 * LORA_PAD], a2_ref[...]), _dot(lora_scr[rs, 2 * LORA_PAD:], g2_ref[...]))

    def finish(rs, r, k, v, zw, za, g):
        z = -(w0 + zw)
        softplus = jnp.maximum(z, 0.0) + jnp.log(1.0 + jnp.exp(-jnp.abs(z)))
        lw = -jnp.exp(-softplus - 0.5)
        a = _sigmoid(a0 + za)
        kk = k * k_k
        nrm = jnp.sqrt(_segsum(kk * kk, seg))
        kk = kk / jnp.maximum(nrm, 1e-12)
        k2 = k * (1.0 + (a - 1.0) * k_a)
        bonus = _segsum(r * k2 * r_k, seg) * v
        r_out[0, rs] = r.astype(BF16)
        lw_out[0, rs] = lw
        k_out[0, rs] = k2.astype(BF16)
        v_out[0, rs] = v.astype(BF16)
        kk_out[0, rs] = kk.astype(BF16)
        a_out[0, rs] = a.astype(BF16)
        g_out[0, rs] = g.astype(BF16)
        bonus_out[0, rs] = bonus.astype(BF16)

    halves = [slice(0, tm // 2), slice(tm // 2, tm)]
    projected = [projections(rs) for rs in halves]
    for rs, p in zip(halves, projected):
        finish(rs, *p)


def _rwkv_in(x, sc, sh, mu, wl1, w_r, w_k, w_v, w2, a2, g2, vec, seg, tm=512, tn=256):
    b, s, d = x.shape
    sub = 8
    vecs = pl.BlockSpec((1, 1, d), lambda bi, i, j: (bi, 0, 0))
    col = lambda rows: pl.BlockSpec((rows, tn), lambda bi, i, j: (0, j))
    out = pl.BlockSpec((1, tm, tn), lambda bi, i, j: (bi, i, j))
    return pl.pallas_call(
        _rwkv_in_kernel,
        grid=(b, s // tm, d // tn),
        in_specs=[pl.BlockSpec((1, tm, d), lambda bi, i, j: (bi, i, 0)),
                  pl.BlockSpec((1, sub, d), lambda bi, i, j: (bi, jnp.maximum(i * (tm // sub) - 1, 0), 0)),
                  vecs, vecs,
                  pl.BlockSpec(mu.shape, lambda bi, i, j: (0, 0)),
                  pl.BlockSpec(wl1.shape, lambda bi, i, j: (0, 0)),
                  col(d), col(d), col(d), col(LORA_PAD), col(LORA_PAD), col(R_GATE_LORA),
                  col(8), pl.BlockSpec(seg.shape, lambda bi, i, j: (0, 0))],
        out_specs=[out] * 8,
        out_shape=[jax.ShapeDtypeStruct((b, s, d), F32 if i == 1 else BF16) for i in range(8)],
        scratch_shapes=[pltpu.VMEM((6, tm, d), BF16), pltpu.VMEM((tm, 2 * LORA_PAD + R_GATE_LORA), BF16)],
        compiler_params=_params("parallel", "parallel", "arbitrary"),
    )(x, x, sc, sh, mu, wl1, w_r, w_k, w_v, w2, a2, g2, vec, seg)


def _rwkv_rec_kernel(r_ref, lw_ref, k_ref, v_ref, kk_ref, a_ref, o_ref, s_scr, *, n_chunks, n_heads):
    t = pl.program_id(2)

    @pl.when(t == 0)
    def _():
        s_scr[...] = jnp.zeros_like(s_scr)

    c_len = CHUNK
    n = R_HEAD
    ri = lax.broadcasted_iota(jnp.int32, (c_len, c_len), 0)
    ci = lax.broadcasted_iota(jnp.int32, (c_len, c_len), 1)
    strict = ri > ci
    incl = ri >= ci
    tri = jnp.where(incl, 1.0, 0.0).astype(BF16)
    items = [(c, h) for c in range(n_chunks) for h in range(n_heads)]

    pre = []
    for c in range(n_chunks):
        sl = slice(c * c_len, (c + 1) * c_len)
        lw = lw_ref[0, sl, :]
        k = k_ref[0, sl, :].astype(F32)
        kk = kk_ref[0, sl, :].astype(F32)
        kb = kk * a_ref[0, sl, :].astype(F32)
        cum = _cumsum_rows(tri, lw)
        cum_c = cum[c_len - 1:c_len, :]
        p_inv = jnp.exp(-cum)
        p_hat = jnp.exp(cum_c - cum)
        pre.append(dict(
            r_t=r_ref[0, sl, :].astype(F32) * jnp.exp(cum), a_t=-kk * jnp.exp(cum - lw), b_t=kb * p_inv,
            k_t=k * p_inv, b_h=kb * p_hat, k_h=k * p_hat, p_c=jnp.exp(cum_c), v=v_ref[0, sl, :].astype(F32)))

    def head(c, h, name):
        return pre[c][name][:, h * n:(h + 1) * n]

    amat = {}
    for c, h in items:
        ar = jnp.concatenate([head(c, h, "a_t"), head(c, h, "r_t")], axis=0)
        bk = jnp.concatenate([head(c, h, "b_t"), head(c, h, "k_t")], axis=0)
        amat[c, h] = _mm(ar, bk, NT_DIMS, PASSES_PAIR)
    l_ab, m_rb, lv = {}, {}, {}
    for c, h in items:
        am = amat[c, h]
        l_ab[c, h] = jnp.where(strict, am[:c_len, :c_len], 0.0)
        m_rb[c, h] = jnp.where(incl, am[c_len:, :c_len], 0.0)
        lm = jnp.concatenate([jnp.where(strict, am[:c_len, c_len:], 0.0),
                              jnp.where(incl, am[c_len:, c_len:], 0.0)], axis=0)
        lv[c, h] = _mm(lm, head(c, h, "v"), NN_DIMS, PASSES_PAIR)
    z = {}
    for c, h in items:
        z[c, h] = jnp.concatenate([head(c, h, "a_t"), lv[c, h][:c_len], l_ab[c, h]], axis=1)
    for it in range(6):
        for c, h in items:
            zz = z[c, h]
            pw = zz[:, 2 * n:]
            if it < 5:
                x = _mm(pw, zz, NN_DIMS, PASSES_SOLVE)
                z[c, h] = jnp.concatenate([zz[:, :2 * n] + x[:, :2 * n], x[:, 2 * n:]], axis=1)
            else:
                z[c, h] = zz[:, :2 * n] + _mm(pw, zz[:, :2 * n], NN_DIMS, PASSES_SOLVE)
    state = [s_scr[h] for h in range(n_heads)]
    for c in range(n_chunks):
        u = [_mm(z[c, h][:, :n], state[h], NT_DIMS, PASSES_STATE) + z[c, h][:, n:] for h in range(n_heads)]
        o = [_mm(head(c, h, "r_t"), state[h], NT_DIMS, PASSES_STATE) + _mm(m_rb[c, h], u[h], NN_DIMS, PASSES_STATE)
             + lv[c, h][c_len:] for h in range(n_heads)]
        state = [state[h] * head(c, h, "p_c")
                 + _mm(u[h], head(c, h, "b_h"), TN_DIMS, PASSES_STATE)
                 + _mm(head(c, h, "v"), head(c, h, "k_h"), TN_DIMS, PASSES_STATE) for h in range(n_heads)]
        o_ref[0, c * c_len:(c + 1) * c_len, :] = jnp.concatenate(o, axis=1)
    for h in range(n_heads):
        s_scr[h] = state[h]


def _mm(a, b, dims=NN_DIMS, passes=1):
    if passes == 1:
        return _dot(a.astype(BF16), b.astype(BF16), dims)
    return _dot3(a, b, dims)


def _cumsum_rows(tri, x):
    hi, lo = _split(x)
    lo2 = (x - hi.astype(F32) - lo.astype(F32)).astype(BF16)
    return _dot(tri, hi) + _dot(tri, lo) + _dot(tri, lo2)


def _rwkv_rec(r, lw, k, v, kk, a, tb=256, lanes=512):
    b, s, d = r.shape
    tb = min(tb, s)
    n_heads = lanes // R_HEAD
    kern = functools.partial(_rwkv_rec_kernel, n_chunks=tb // CHUNK, n_heads=n_heads)
    blk = pl.BlockSpec((1, tb, lanes), lambda bi, hi, ti: (bi, ti, hi))
    return pl.pallas_call(
        kern,
        grid=(b, d // lanes, s // tb),
        in_specs=[blk] * 6,
        out_specs=blk,
        out_shape=jax.ShapeDtypeStruct((b, s, d), F32),
        scratch_shapes=[pltpu.VMEM((n_heads, R_HEAD, R_HEAD), F32)],
        compiler_params=_params("parallel", "parallel", "arbitrary"),
    )(r, lw, k, v, kk, a)


def _rwkv_out_kernel(o_ref, g_ref, bonus_ref, x_ref, gate_ref, gng_ref, gnb_ref, seg_ref, wo_ref,
                     out_ref, y_scr):
    d = x_ref.shape[2]
    tn = seg_ref.shape[0]
    seg = seg_ref[...]
    inv_n = 1.0 / R_HEAD
    tm = x_ref.shape[1]
    for rs in (slice(0, tm // 2), slice(tm // 2, tm)):
        for jt in range(d // tn):
            sl = slice(jt * tn, (jt + 1) * tn)
            o = o_ref[0, rs, sl]
            mu = _segsum(o, seg) * inv_n
            dv = o - mu
            var = _segsum(dv * dv, seg) * inv_n
            ln = dv * lax.rsqrt(var + R_GN_EPS)
            y = ((ln * gng_ref[:, sl] + gnb_ref[:, sl] + bonus_ref[0, rs, sl].astype(F32))
                 * g_ref[0, rs, sl].astype(F32))
            y_scr[rs, sl] = y.astype(BF16)
        out_ref[0, rs] = x_ref[0, rs] + gate_ref[0] * _dot(y_scr[rs], wo_ref[...])


def _rwkv_out(o, g, bonus, x, gate, gng, gnb, seg, w_o, tm=512):
    b, s, d = x.shape
    tok = pl.BlockSpec((1, tm, d), lambda bi, i: (bi, i, 0))
    full = lambda a: pl.BlockSpec(a.shape, lambda bi, i: (0,) * a.ndim)
    return pl.pallas_call(
        _rwkv_out_kernel,
        grid=(b, s // tm),
        in_specs=[tok, tok, tok, tok, pl.BlockSpec((1, 1, d), lambda bi, i: (bi, 0, 0)),
                  full(gng), full(gnb), full(seg), full(w_o)],
        out_specs=tok,
        out_shape=jax.ShapeDtypeStruct((b, s, d), F32),
        scratch_shapes=[pltpu.VMEM((tm, d), BF16)],
        compiler_params=_params("parallel", "parallel"),
    )(o, g, bonus, x, gate, gng, gnb, seg, w_o)


def _rwkv_mixer(x, sc, sh, gate, mu, w_r, w_k, w_v, w_o, w0, w_w1, w_w2, a0, w_a1, w_a2, w_g1, w_g2,
                k_k, k_a, r_k, gn_g, gn_b):
    d = x.shape[2]
    tn = 256
    lora = w_w1.shape[1]
    padc = lambda w: jnp.pad(w, ((0, 0), (0, LORA_PAD - lora)))
    padr = lambda w: jnp.pad(w, ((0, LORA_PAD - lora), (0, 0)))
    wl1 = jnp.concatenate([padc(w_w1), padc(w_a1), w_g1], axis=1).astype(BF16)
    vec = jnp.stack([w0, a0, k_k, k_a, r_k.reshape(-1), w0 * 0, w0 * 0, w0 * 0])
    idx = jnp.arange(tn) // R_HEAD
    seg = (idx[:, None] == idx[None, :]).astype(BF16)
    r, lw, k2, v, kk, a, g, bonus = _rwkv_in(
        x, sc, sh, mu, wl1, _to_bf16(w_r[None])[0], _to_bf16(w_k[None])[0], _to_bf16(w_v[None])[0],
        padr(w_w2).astype(BF16), padr(w_a2).astype(BF16), w_g2.astype(BF16), vec, seg, tn=tn)
    o = _rwkv_rec(r, lw, k2, v, kk, a)
    return _rwkv_out(o, g, bonus, x, gate, gn_g.reshape(1, d), gn_b.reshape(1, d), seg, _to_bf16(w_o[None])[0])


def kernel(x, c, positions, ada_w, ada_b, mlp_w1, mlp_w2, final_g, a_w_in, a_q_norm_g, a_kv_norm_g, a_w_uq,
           a_w_qidx, a_kidx_ln_g, a_kidx_ln_b, a_w_uk, a_w_uv, a_w_o, b_mu, b_w_r, b_w_k, b_w_v, b_w_o, b_w0,
           b_w_w1, b_w_w2, b_a0, b_w_a1, b_w_a2, b_w_g1, b_w_g2, b_k_k, b_k_a, b_r_k, b_gn_g, b_gn_b):
    depth = ada_w.shape[0]
    d = x.shape[2]
    mod = _ada_mod(c, ada_w, ada_b)
    w1_all, w2_all = _to_bf16(mlp_w1), _to_bf16(mlp_w2)
    for i in range(depth):
        sh1, sc1, g1, sh2, sc2, g2 = [mod[i, :, None, m * d:(m + 1) * d] for m in range(6)]
        j = i // 2
        if i % 2 == 0:
            x = _dsa_mixer(x, sc1, sh1, g1, positions, a_w_in[j], a_q_norm_g[j], a_kv_norm_g[j], a_w_uq[j],
                           a_w_qidx[j], a_kidx_ln_g[j], a_kidx_ln_b[j], a_w_uk[j], a_w_uv[j], a_w_o[j])
        else:
            x = _rwkv_mixer(x, sc1, sh1, g1, b_mu[j], b_w_r[j], b_w_k[j], b_w_v[j], b_w_o[j], b_w0[j],
                            b_w_w1[j], b_w_w2[j], b_a0[j], b_w_a1[j], b_w_a2[j], b_w_g1[j], b_w_g2[j],
                            b_k_k[j], b_k_a[j], b_r_k[j], b_gn_g[j], b_gn_b[j])
        x = _mlp(x, sc2, sh2, g2, w1_all, w2_all, i, final_g,
                 final_norm=(i == depth - 1))
    return x
```

```python
import functools

import jax
import jax.numpy as jnp
from jax import lax
from jax.experimental import pallas as pl
from jax.experimental.pallas import tpu as pltpu

F32 = jnp.float32
BF16 = jnp.bfloat16

D_MODEL = 2048
DEPTH = 2
CHUNK = 64
EPS = 1e-6
A_HEADS = 16
A_NOPE = 128
A_ROPE = 64
A_V = 128
A_Q_RANK = 512
A_KV_RANK = 256
A_KEY = A_KV_RANK + A_ROPE
IDX_HEADS = 16
IDX_DIM = 64
IDX_ROPE = 32
TOPK_MAX = 256
Q_BLOCK = 128
ROPE_BASE = 10000.0
A_SCALE = (A_NOPE + A_ROPE) ** -0.5
A_IN = A_Q_RANK + A_KV_RANK + A_ROPE + IDX_DIM + IDX_HEADS
A_IN_PAD = 1024
R_HEAD = 64
R_GN_EPS = R_HEAD * 1e-5
LORA_PAD = 128
R_GATE_LORA = 256

VMEM_LIMIT_BYTES = 56 * 1024 * 1024
MASK_VALUE = -1e30

NT_DIMS = (((1,), (1,)), ((), ()))
TN_DIMS = (((0,), (0,)), ((), ()))
NN_DIMS = (((1,), (0,)), ((), ()))

PASSES_PAIR = 1
PASSES_SOLVE = 1
PASSES_STATE = 1


def _params(*sem):
    return pltpu.CompilerParams(dimension_semantics=sem, vmem_limit_bytes=VMEM_LIMIT_BYTES)


def _dot(a, b, dims=NN_DIMS):
    return lax.dot_general(a, b, dims, preferred_element_type=F32)


def _split(x):
    hi = x.astype(BF16)
    lo = (x - hi.astype(F32)).astype(BF16)
    return hi, lo


def _dot3(a, b, dims=NN_DIMS):
    ah, al = _split(a)
    bh, bl = _split(b)
    return _dot(ah, bh, dims) + _dot(ah, bl, dims) + _dot(al, bh, dims)


def _segsum(x, seg):
    return _dot(x.astype(BF16), seg)


def _cast_kernel(x_ref, o_ref):
    o_ref[...] = x_ref[...].astype(BF16)


def _to_bf16(w, block_bytes=8 * 1024 * 1024):
    l, r, c = w.shape
    tr = min(r, max(8, block_bytes // (4 * c)))
    assert r % tr == 0
    blk = pl.BlockSpec((1, tr, c), lambda li, i: (li, i, 0))
    return pl.pallas_call(
        _cast_kernel, grid=(l, r // tr), in_specs=[blk], out_specs=blk,
        out_shape=jax.ShapeDtypeStruct(w.shape, BF16),
        compiler_params=_params("parallel", "parallel"),
    )(w)


def _rms(x):
    return x * lax.rsqrt(jnp.mean(x * x, axis=-1, keepdims=True) + EPS)


def _rms_mod(x, sc, sh):
    return _rms(x) * (1.0 + sc) + sh


def _sigmoid(x):
    return 1.0 / (1.0 + jnp.exp(-x))


def _rope_half(x, c, s):
    d = x.shape[-1] // 2
    x1 = x[:, :d]
    x2 = x[:, d:]
    return jnp.concatenate([x1 * c - x2 * s, x2 * c + x1 * s], axis=-1)


def _mod_kernel(c_ref, w_ref, b_ref, o_ref):
    c = c_ref[...]
    a = (c * _sigmoid(c)).astype(BF16)
    o_ref[0] = _dot(a, w_ref[0].astype(BF16)) + b_ref[0]


def _ada_mod(c, ada_w, ada_b):
    depth, d, n = ada_w.shape
    b = c.shape[0]
    tn = 1024
    return pl.pallas_call(
        _mod_kernel,
        grid=(depth, n // tn),
        in_specs=[
            pl.BlockSpec((b, d), lambda l, j: (0, 0)),
            pl.BlockSpec((1, d, tn), lambda l, j: (l, 0, j)),
            pl.BlockSpec((1, 1, tn), lambda l, j: (l, 0, j)),
        ],
        out_specs=pl.BlockSpec((1, b, tn), lambda l, j: (l, 0, j)),
        out_shape=jax.ShapeDtypeStruct((depth, b, n), F32),
        compiler_params=_params("arbitrary", "arbitrary"),
    )(c, ada_w, ada_b.reshape(depth, 1, n))


def _mlp_kernel(x_ref, xn_ref, sc_ref, sh_ref, scn_ref, shn_ref, g_ref, w1_ref, w2_ref, fg_ref, o_ref,
                h_scr, acc_scr, *, final_norm):
    f = pl.program_id(2)
    last = pl.num_programs(2) - 1

    @pl.when((pl.program_id(0) == 0) & (pl.program_id(1) == 0) & (f == 0))
    def _():
        h_scr[...] = _rms_mod(x_ref[0], sc_ref[0], sh_ref[0]).astype(BF16)
        acc_scr[...] = jnp.zeros_like(acc_scr)

    def hidden():
        a = jnp.maximum(_dot(h_scr[...], w1_ref[0]), 0.0)
        return _dot((a * a).astype(BF16), w2_ref[0])

    @pl.when(f < last)
    def _():
        acc_scr[...] += hidden()

    @pl.when(f == last)
    def _():
        y = x_ref[0] + g_ref[0] * (acc_scr[...] + hidden())
        if final_norm:
            y = _rms(y) * fg_ref[...]
        o_ref[0] = y
        h_scr[...] = _rms_mod(xn_ref[0], scn_ref[0], shn_ref[0]).astype(BF16)
        acc_scr[...] = jnp.zeros_like(acc_scr)


def _mlp(x, sc, sh, g, w1, w2, layer, final_g, final_norm, tm=512, tf=1024):
    b, s, d = x.shape
    dff = w1.shape[2]
    n_i = s // tm
    kern = functools.partial(_mlp_kernel, final_norm=final_norm)

    def next_block(bi, i):
        return jnp.minimum(bi * n_i + i + 1, b * n_i - 1)

    vec = pl.BlockSpec((1, 1, d), lambda bi, i, f: (bi, 0, 0))
    vec_next = pl.BlockSpec((1, 1, d), lambda bi, i, f: (next_block(bi, i) // n_i, 0, 0))
    return pl.pallas_call(
        kern,
        grid=(b, n_i, dff // tf),
        in_specs=[
            pl.BlockSpec((1, tm, d), lambda bi, i, f: (bi, i, 0)),
            pl.BlockSpec((1, tm, d), lambda bi, i, f: (next_block(bi, i) // n_i, next_block(bi, i) % n_i, 0)),
            vec, vec, vec_next, vec_next, vec,
            pl.BlockSpec((1, d, tf), lambda bi, i, f: (layer, 0, f)),
            pl.BlockSpec((1, tf, d), lambda bi, i, f: (layer, f, 0)),
            pl.BlockSpec((1, d), lambda bi, i, f: (0, 0)),
        ],
        out_specs=pl.BlockSpec((1, tm, d), lambda bi, i, f: (bi, i, 0)),
        out_shape=jax.ShapeDtypeStruct((b, s, d), F32),
        scratch_shapes=[pltpu.VMEM((tm, d), BF16), pltpu.VMEM((tm, d), F32)],
        compiler_params=_params("arbitrary", "arbitrary", "arbitrary"),
    )(x, x, sc, sh, sc, sh, g, w1, w2, final_g.reshape(1, d))


def _dsa_in_kernel(x_ref, sc_ref, sh_ref, w_ref, qg_ref, kvg_ref, lng_ref, lnb_ref,
                   ca_ref, sa_ref, ci_ref, si_ref, cq_ref, keys_ref, kidx_ref, widx_ref):
    o_kv = A_Q_RANK
    o_kr = o_kv + A_KV_RANK
    o_ki = o_kr + A_ROPE
    o_wi = o_ki + IDX_DIM
    tm = x_ref.shape[1]
    halves = [slice(0, tm // 2), slice(tm // 2, tm)]
    hs = [_rms_mod(x_ref[0, rs], sc_ref[0], sh_ref[0]).astype(BF16) for rs in halves]
    projs = [_dot(h, w_ref[...]) for h in hs]
    for rs, proj in zip(halves, projs):
        cq_ref[0, rs] = (_rms(proj[:, :o_kv]) * qg_ref[...]).astype(BF16)
        keys_ref[0, rs, :A_KV_RANK] = (_rms(proj[:, o_kv:o_kr]) * kvg_ref[...]).astype(BF16)
        keys_ref[0, rs, A_KV_RANK:] = _rope_half(proj[:, o_kr:o_ki], ca_ref[0, rs], sa_ref[0, rs]).astype(BF16)
        ki = proj[:, o_ki:o_wi]
        mu = jnp.mean(ki, axis=-1, keepdims=True)
        kc = ki - mu
        var = jnp.mean(kc * kc, axis=-1, keepdims=True)
        ki = kc * lax.rsqrt(var + EPS) * lng_ref[...] + lnb_ref[...]
        ki = jnp.concatenate([_rope_half(ki[:, :IDX_ROPE], ci_ref[0, rs], si_ref[0, rs]), ki[:, IDX_ROPE:]],
                             axis=-1)
        kidx_ref[0, rs] = ki.astype(BF16)
        widx_ref[0, rs] = proj[:, o_wi:o_wi + IDX_HEADS] * (IDX_HEADS ** -0.5 * IDX_DIM ** -0.5)


def _dsa_in(x, sc, sh, w_in_p, qg, kvg, lng, lnb, cos_a, sin_a, cos_i, sin_i, tm=512):
    b, s, d = x.shape
    vec = pl.BlockSpec((1, 1, d), lambda bi, i: (bi, 0, 0))

    def full(a):
        return pl.BlockSpec(a.shape, lambda bi, i: (0,) * a.ndim)

    def tok(w):
        return pl.BlockSpec((1, tm, w), lambda bi, i: (bi, i, 0))

    return pl.pallas_call(
        _dsa_in_kernel,
        grid=(b, s // tm),
        in_specs=[tok(d), vec, vec, full(w_in_p), full(qg), full(kvg), full(lng), full(lnb),
                  tok(A_ROPE // 2), tok(A_ROPE // 2), tok(IDX_ROPE // 2), tok(IDX_ROPE // 2)],
        out_specs=[tok(A_Q_RANK), tok(A_KEY), tok(IDX_DIM), tok(IDX_HEADS)],
        out_shape=[jax.ShapeDtypeStruct((b, s, A_Q_RANK), BF16),
                   jax.ShapeDtypeStruct((b, s, A_KEY), BF16),
                   jax.ShapeDtypeStruct((b, s, IDX_DIM), BF16),
                   jax.ShapeDtypeStruct((b, s, IDX_HEADS), F32)],
        compiler_params=_params("parallel", "parallel"),
    )(x, sc, sh, w_in_p, qg, kvg, lng, lnb, cos_a, sin_a, cos_i, sin_i)


def _fold_kernel(a_ref, b_ref, o_ref):
    o_ref[0] = _dot3(a_ref[0], b_ref[0]).astype(BF16)


def _fold_uq_uk(w_uq_nope, w_uk_t):
    h, rq, dn = w_uq_nope.shape
    rkv = w_uk_t.shape[2]
    return pl.pallas_call(
        _fold_kernel,
        grid=(h,),
        in_specs=[pl.BlockSpec((1, rq, dn), lambda i: (i, 0, 0)),
                  pl.BlockSpec((1, dn, rkv), lambda i: (i, 0, 0))],
        out_specs=pl.BlockSpec((1, rq, rkv), lambda i: (i, 0, 0)),
        out_shape=jax.ShapeDtypeStruct((h, rq, rkv), BF16),
        compiler_params=_params("arbitrary"),
    )(w_uq_nope, w_uk_t)


def _dsa_q_kernel(cq_ref, w_ref, ca_ref, sa_ref, ci_ref, si_ref, qatt_ref, qidx_ref):
    cq = cq_ref[0]
    tm = cq.shape[0]
    ca, sa, ci, si = ca_ref[0], sa_ref[0], ci_ref[0], si_ref[0]
    one = jnp.ones((tm, IDX_DIM - IDX_ROPE), F32)
    cq_t = jnp.concatenate([ca, ca, ca, ca], axis=-1)
    sq_t = jnp.concatenate([-sa, sa, -sa, sa], axis=-1)
    ci_t = jnp.concatenate([ci, ci, one, ci, ci, one], axis=-1)
    si_t = jnp.concatenate([-si, si, 0.0 * one, -si, si, 0.0 * one], axis=-1)
    nb = tm // Q_BLOCK
    n_lat = A_HEADS * A_KV_RANK
    n_rope = A_HEADS * A_ROPE
    n_idx = IDX_HEADS * IDX_DIM
    for h in range(A_HEADS):
        lat = _dot(cq, w_ref[:, h * A_KV_RANK:(h + 1) * A_KV_RANK]) * A_SCALE
        qatt_ref[0, :, h, :, :A_KV_RANK] = lat.astype(BF16).reshape(nb, Q_BLOCK, A_KV_RANK)
    o = n_lat
    r = _dot(cq, w_ref[:, o:o + n_rope])
    rs = _dot(cq, w_ref[:, o + n_rope:o + 2 * n_rope])
    rep = n_rope // cq_t.shape[1]
    qr = ((r * jnp.tile(cq_t, (1, rep)) + rs * jnp.tile(sq_t, (1, rep))) * A_SCALE).astype(BF16)
    for h in range(A_HEADS):
        qatt_ref[0, :, h, :, A_KV_RANK:] = qr[:, h * A_ROPE:(h + 1) * A_ROPE].reshape(nb, Q_BLOCK, A_ROPE)
    o = n_lat + 2 * n_rope
    qi = _dot(cq, w_ref[:, o:o + n_idx])
    qis = _dot(cq, w_ref[:, o + n_idx:o + 2 * n_idx])
    rep = n_idx // ci_t.shape[1]
    qi = (qi * jnp.tile(ci_t, (1, rep)) + qis * jnp.tile(si_t, (1, rep))).astype(BF16)
    for h in range(IDX_HEADS):
        qidx_ref[0, :, h] = qi[:, h * IDX_DIM:(h + 1) * IDX_DIM].reshape(nb, Q_BLOCK, IDX_DIM)


def _dsa_q(cq, w_q, cos_a, sin_a, cos_i, sin_i, tm=512):
    b, s, rq = cq.shape
    nb = tm // Q_BLOCK

    def tok(w):
        return pl.BlockSpec((1, tm, w), lambda bi, i: (bi, i, 0))

    return pl.pallas_call(
        _dsa_q_kernel,
        grid=(b, s // tm),
        in_specs=[tok(rq), pl.BlockSpec(w_q.shape, lambda bi, i: (0, 0)),
                  tok(A_ROPE // 2), tok(A_ROPE // 2), tok(IDX_ROPE // 2), tok(IDX_ROPE // 2)],
        out_specs=[pl.BlockSpec((1, nb, A_HEADS, Q_BLOCK, A_KEY), lambda bi, i: (bi, i, 0, 0, 0)),
                   pl.BlockSpec((1, nb, IDX_HEADS, Q_BLOCK, IDX_DIM), lambda bi, i: (bi, i, 0, 0, 0))],
        out_shape=[jax.ShapeDtypeStruct((b, s // Q_BLOCK, A_HEADS, Q_BLOCK, A_KEY), BF16),
                   jax.ShapeDtypeStruct((b, s // Q_BLOCK, IDX_HEADS, Q_BLOCK, IDX_DIM), BF16)],
        compiler_params=_params("parallel", "parallel"),
    )(cq, w_q, cos_a, sin_a, cos_i, sin_i)


KEY_NEG_INF = -2139095041


def _key_to_float(key):
    return pltpu.bitcast(key ^ ((key >> 31) & jnp.int32(0x7FFFFFFF)), F32)


def _dsa_idx_kernel(qidx_ref, kidx_ref, widx_ref, bias_ref, score_scr, last_scr, *, topk, tk):
    qb = pl.program_id(1)
    s = kidx_ref.shape[1]
    n_tiles = ((qb + 1) * Q_BLOCK + tk - 1) // tk
    q_all = qidx_ref[0, 0].reshape(IDX_HEADS * Q_BLOCK, IDX_DIM)
    w = widx_ref[0]
    q_chunk = (qb * Q_BLOCK + lax.broadcasted_iota(jnp.int32, (tk, Q_BLOCK), 1)) // CHUNK
    key0 = lax.broadcasted_iota(jnp.int32, (tk, Q_BLOCK), 0)
    sub = min(tk, 128)

    def tile_off(t):
        return pl.multiple_of(t * tk, tk)

    def allowed_tile(off):
        return ((key0 + off) // CHUNK) <= q_chunk

    def score_tile(t, carry):
        off = tile_off(t)
        allowed = allowed_tile(off)
        for j in range(tk // sub):
            rows = slice(j * sub, (j + 1) * sub)
            rel = _dot(kidx_ref[0, pl.ds(off + j * sub, sub), :], q_all, NT_DIMS)
            sc = jnp.zeros((sub, Q_BLOCK), F32)
            for h in range(IDX_HEADS):
                sc = sc + jnp.maximum(rel[:, h * Q_BLOCK:(h + 1) * Q_BLOCK], 0.0) * w[h:h + 1, :]
            score_scr[pl.ds(off + j * sub, sub), :] = jnp.where(allowed[rows], sc, -jnp.inf)
        return carry

    lax.fori_loop(0, n_tiles, score_tile, 0)

    cnt_rows = 64

    def count(pred):
        def body(t, acc):
            off = tile_off(t)
            hit = jnp.where(pred(score_scr[pl.ds(off, tk), :], off), 1.0, 0.0)
            for j in range(tk // cnt_rows):
                acc = acc + hit[j * cnt_rows:(j + 1) * cnt_rows]
            return acc

        acc = lax.fori_loop(0, n_tiles, body, jnp.zeros((cnt_rows, Q_BLOCK), F32))
        return jnp.sum(acc, axis=0, keepdims=True)

    kf = jnp.float32(topk)
    n_visited = (n_tiles * tk).astype(F32)

    def count_key(cand):
        cand_f = _key_to_float(cand)
        return jnp.where(cand < KEY_NEG_INF, n_visited, count(lambda sc, off: sc >= cand_f))

    thr = jnp.where(count_key(jnp.zeros((1, Q_BLOCK), jnp.int32)) >= kf, jnp.int32(0), jnp.int32(-2 ** 31))

    def bit_step(i, thr):
        cand = thr | lax.shift_left(jnp.int32(1), jnp.int32(30) - i)
        return jnp.where(count_key(cand) >= kf, cand, thr)

    thr = lax.fori_loop(0, 31, bit_step, thr)
    thr_f = _key_to_float(thr)
    take_all = thr < KEY_NEG_INF

    last_scr[...] = jnp.full((1, Q_BLOCK), s - 1, jnp.int32)
    n_ge = jnp.where(take_all, kf, count(lambda sc, off: sc >= thr_f))

    @pl.when(jnp.max(n_ge) > kf)
    def _():
        need = kf - count(lambda sc, off: sc > thr_f)

        def idx_step(i, last):
            cand = last | lax.shift_left(jnp.int32(1), jnp.int32((s - 1).bit_length() - 1) - i)
            below = count(lambda sc, off: (sc == thr_f) & (key0 + off < cand))
            return jnp.where(below < need, cand, last)

        last = lax.fori_loop(0, (s - 1).bit_length(), idx_step, jnp.zeros((1, Q_BLOCK), jnp.int32))
        last_scr[...] = jnp.where(n_ge > kf, last, s - 1)

    last = last_scr[...]

    def write_tile(t, carry):
        off = tile_off(t)
        sc = score_scr[pl.ds(off, tk), :]
        sel = ((sc > thr_f) | ((sc == thr_f) & (key0 + off <= last)) | take_all) & allowed_tile(off)
        bias_ref[0, :, pl.ds(off, tk)] = jnp.where(sel, 0.0, MASK_VALUE).T.astype(BF16)
        return carry

    lax.fori_loop(0, n_tiles, write_tile, 0)

    def mask_tile(t, carry):
        bias_ref[0, :, pl.ds(tile_off(t), tk)] = jnp.full((Q_BLOCK, tk), MASK_VALUE, BF16)
        return carry

    lax.fori_loop(n_tiles, s // tk, mask_tile, 0)


def _dsa_idx(qidx, kidx, widx_t, topk, tk=256):
    b, nb = qidx.shape[:2]
    s = kidx.shape[1]
    tk = min(tk, s)
    kern = functools.partial(_dsa_idx_kernel, topk=topk, tk=tk)
    return pl.pallas_call(
        kern,
        grid=(b, nb),
        in_specs=[pl.BlockSpec((1, 1, IDX_HEADS, Q_BLOCK, IDX_DIM), lambda bi, i: (bi, i, 0, 0, 0)),
                  pl.BlockSpec((1, s, IDX_DIM), lambda bi, i: (bi, 0, 0)),
                  pl.BlockSpec((1, IDX_HEADS, Q_BLOCK), lambda bi, i: (bi, 0, i))],
        out_specs=pl.BlockSpec((1, Q_BLOCK, s), lambda bi, i: (bi, i, 0)),
        out_shape=jax.ShapeDtypeStruct((b, s, s), BF16),
        scratch_shapes=[pltpu.VMEM((s, Q_BLOCK), F32), pltpu.VMEM((1, Q_BLOCK), jnp.int32)],
        compiler_params=_params("parallel", "parallel"),
    )(qidx, kidx, widx_t)


def _dsa_attn_kernel(q_ref, keys_ref, bias_ref, o_ref, m_scr, l_scr, acc_scr, *, tk, n_strips):
    qb = pl.program_id(1)
    hs = A_HEADS // n_strips
    rows = hs * Q_BLOCK
    n_tiles = ((qb + 1) * Q_BLOCK + tk - 1) // tk

    def tile_step(t, first):
        off = pl.multiple_of(t * tk, tk)
        kt = keys_ref[0, pl.ds(off, tk), :]
        kv = kt[:, :A_KV_RANK]
        bias = bias_ref[0, :, pl.ds(off, tk)].astype(F32)

        def scores(i):
            q = q_ref[0, 0, i * hs:(i + 1) * hs].reshape(rows, A_KEY)
            return _dot(q, kt, NT_DIMS)

        def softmax_update(i, sc):
            rs = slice(i * rows, (i + 1) * rows)
            sc = (sc.reshape(hs, Q_BLOCK, tk) + bias[None]).reshape(rows, tk)
            m_cur = jnp.max(sc, axis=1, keepdims=True)
            if first:
                m_new = jnp.broadcast_to(m_cur, (rows, 128))
                alpha = None
            else:
                m_prev = m_scr[rs]
                m_new = jnp.maximum(m_prev, m_cur)
                alpha = jnp.exp(m_prev - m_new)
            p = jnp.exp(sc - jnp.tile(m_new, (1, tk // 128)))
            l_cur = jnp.sum(p, axis=1, keepdims=True)
            l_scr[rs] = jnp.broadcast_to(l_cur, (rows, 128)) if first else alpha * l_scr[rs] + l_cur
            m_scr[rs] = m_new
            return p.astype(BF16), alpha

        def values(i, p, alpha):
            rs = slice(i * rows, (i + 1) * rows)
            pv = _dot(p, kv)
            acc_scr[rs] = pv if first else acc_scr[rs] * jnp.tile(alpha, (1, A_KV_RANK // 128)) + pv

        sc = {0: scores(0)}
        pa = {}
        for i in range(n_strips):
            if i + 1 < n_strips:
                sc[i + 1] = scores(i + 1)
            pa[i] = softmax_update(i, sc.pop(i))
            if i > 0:
                values(i - 1, *pa.pop(i - 1))
        values(n_strips - 1, *pa.pop(n_strips - 1))

    tile_step(0, True)

    def body(t, carry):
        tile_step(t, False)
        return carry

    lax.fori_loop(1, n_tiles, body, 0)
    o = acc_scr[...] / jnp.tile(l_scr[...], (1, A_KV_RANK // 128))
    o_ref[0, 0] = o.astype(BF16).reshape(A_HEADS, Q_BLOCK, A_KV_RANK)


def _dsa_attn(qatt, keys, bias, tk=512, n_strips=8):
    b, nb = qatt.shape[:2]
    s = keys.shape[1]
    tk = min(tk, s)
    rows = A_HEADS * Q_BLOCK
    kern = functools.partial(_dsa_attn_kernel, tk=tk, n_strips=n_strips)
    return pl.pallas_call(
        kern,
        grid=(b, nb),
        in_specs=[pl.BlockSpec((1, 1, A_HEADS, Q_BLOCK, A_KEY), lambda bi, i: (bi, i, 0, 0, 0)),
                  pl.BlockSpec((1, s, A_KEY), lambda bi, i: (bi, 0, 0)),
                  pl.BlockSpec((1, Q_BLOCK, s), lambda bi, i: (bi, i, 0))],
        out_specs=pl.BlockSpec((1, 1, A_HEADS, Q_BLOCK, A_KV_RANK), lambda bi, i: (bi, i, 0, 0, 0)),
        out_shape=jax.ShapeDtypeStruct((b, nb, A_HEADS, Q_BLOCK, A_KV_RANK), BF16),
        scratch_shapes=[pltpu.VMEM((rows, 128), F32), pltpu.VMEM((rows, 128), F32),
                        pltpu.VMEM((rows, A_KV_RANK), F32)],
        compiler_params=_params("parallel", "parallel"),
    )(qatt, keys, bias)


def _dsa_out_kernel(ol_ref, wuv_ref, wo_ref, x_ref, g_ref, o_ref, y_scr):
    nb = ol_ref.shape[1]
    tm = nb * Q_BLOCK
    for h in range(A_HEADS):
        oh = ol_ref[0, :, h].reshape(tm, A_KV_RANK)
        y_scr[:, h * A_V:(h + 1) * A_V] = _dot(oh, wuv_ref[h]).astype(BF16)
    o_ref[0] = x_ref[0] + g_ref[0] * _dot(y_scr[...], wo_ref[...])


def _dsa_out(olat, w_uv_h, w_o, x, g, tm=512):
    b, s, d = x.shape
    nb = tm // Q_BLOCK
    return pl.pallas_call(
        _dsa_out_kernel,
        grid=(b, s // tm),
        in_specs=[pl.BlockSpec((1, nb, A_HEADS, Q_BLOCK, A_KV_RANK), lambda bi, i: (bi, i, 0, 0, 0)),
                  pl.BlockSpec(w_uv_h.shape, lambda bi, i: (0, 0, 0)),
                  pl.BlockSpec(w_o.shape, lambda bi, i: (0, 0)),
                  pl.BlockSpec((1, tm, d), lambda bi, i: (bi, i, 0)),
                  pl.BlockSpec((1, 1, d), lambda bi, i: (bi, 0, 0))],
        out_specs=pl.BlockSpec((1, tm, d), lambda bi, i: (bi, i, 0)),
        out_shape=jax.ShapeDtypeStruct((b, s, d), F32),
        scratch_shapes=[pltpu.VMEM((tm, A_HEADS * A_V), BF16)],
        compiler_params=_params("parallel", "parallel"),
    )(olat, w_uv_h, w_o, x, g)


def _rope_tables(positions):
    def angles(dim):
        inv = 1.0 / (ROPE_BASE ** (jnp.arange(0, dim, 2, dtype=F32) / dim))
        ang = positions.astype(F32)[..., None] * inv
        return jnp.cos(ang), jnp.sin(ang)

    cos_a, sin_a = angles(A_ROPE)
    cos_i, sin_i = angles(IDX_ROPE)
    return cos_a, sin_a, cos_i, sin_i


def _dsa_mixer(x, sc, sh, g, positions, w_in, qg, kvg, w_uq, w_qidx, lng, lnb, w_uk, w_uv, w_o):
    b, s, d = x.shape
    topk = min(TOPK_MAX, s // 4)
    cos_a, sin_a, cos_i, sin_i = _rope_tables(positions)
    w_in_p = jnp.pad(w_in, ((0, 0), (0, A_IN_PAD - A_IN))).astype(BF16)
    cq, keys, kidx, widx = _dsa_in(x, sc, sh, w_in_p, qg.reshape(1, -1), kvg.reshape(1, -1),
                                   lng.reshape(1, -1), lnb.reshape(1, -1), cos_a, sin_a, cos_i, sin_i)

    rq = w_uq.shape[0]
    w_uq_h = w_uq.reshape(rq, A_HEADS, A_NOPE + A_ROPE)
    w_lat = _fold_uq_uk(jnp.transpose(w_uq_h[:, :, :A_NOPE], (1, 0, 2)), jnp.transpose(w_uk, (1, 2, 0)))
    w_lat = jnp.transpose(w_lat, (1, 0, 2)).reshape(rq, A_HEADS * A_KV_RANK)
    w_r = w_uq_h[:, :, A_NOPE:]
    w_rs = jnp.concatenate([w_r[..., A_ROPE // 2:], w_r[..., :A_ROPE // 2]], axis=-1)
    w_i = w_qidx.reshape(rq, IDX_HEADS, IDX_DIM)
    hr = IDX_ROPE // 2
    w_is = jnp.concatenate([w_i[..., hr:IDX_ROPE], w_i[..., :hr], jnp.zeros_like(w_i[..., IDX_ROPE:])], axis=-1)
    w_q = jnp.concatenate([w_lat, w_r.reshape(rq, -1).astype(BF16), w_rs.reshape(rq, -1).astype(BF16),
                           w_i.reshape(rq, -1).astype(BF16), w_is.reshape(rq, -1).astype(BF16)], axis=1)
    qatt, qidx = _dsa_q(cq, w_q, cos_a, sin_a, cos_i, sin_i)

    bias = _dsa_idx(qidx, kidx, jnp.transpose(widx, (0, 2, 1)), topk)
    olat = _dsa_attn(qatt, keys, bias)
    w_uv_h = jnp.transpose(w_uv, (1, 0, 2)).astype(BF16)
    return _dsa_out(olat, w_uv_h, _to_bf16(w_o[None])[0], x, g)


def _rwkv_in_kernel(x_ref, xp_ref, sc_ref, sh_ref, mu_ref, wl1_ref, wr_ref, wk_ref, wv_ref,
                    w2_ref, a2_ref, g2_ref, vec_ref, seg_ref,
                    r_out, lw_out, k_out, v_out, kk_out, a_out, g_out, bonus_out,
                    mix_scr, lora_scr):
    i = pl.program_id(1)
    j = pl.program_id(2)

    @pl.when(j == 0)
    def _():
        sc = sc_ref[0]
        sh = sh_ref[0]
        h = _rms_mod(x_ref[0], sc, sh)
        hp = _rms_mod(xp_ref[0, 7:8, :], sc, sh)
        hp = jnp.where(i == 0, 0.0, hp)
        rows = lax.broadcasted_iota(jnp.int32, h.shape, 0)
        h_prev = jnp.where(rows == 0, hp, pltpu.roll(h, 1, axis=0))
        delta = h_prev - h
        for m in range(6):
            mix_scr[m] = (h + delta * mu_ref[m:m + 1, :]).astype(BF16)
        tw = jnp.tanh(_dot(mix_scr[1], wl1_ref[:, :LORA_PAD]))
        ta = _dot(mix_scr[4], wl1_ref[:, LORA_PAD:2 * LORA_PAD])
        tg = _sigmoid(_dot(mix_scr[5], wl1_ref[:, 2 * LORA_PAD:]))
        lora_scr[:, :LORA_PAD] = tw.astype(BF16)
        lora_scr[:, LORA_PAD:2 * LORA_PAD] = ta.astype(BF16)
        lora_scr[:, 2 * LORA_PAD:] = tg.astype(BF16)

    w0 = vec_ref[0:1, :]
    a0 = vec_ref[1:2, :]
    k_k = vec_ref[2:3, :]
    k_a = vec_ref[3:4, :]
    r_k = vec_ref[4:5, :]
    seg = seg_ref[...]
    r = _dot(mix_scr[0], wr_ref[...])
    k = _dot(mix_scr[2], wk_ref[...])
    v = _dot(mix_scr[3], wv_ref[...])
    z = -(w0 + _dot(lora_scr[:, :LORA_PAD], w2_ref[...]))
    softplus = jnp.maximum(z, 0.0) + jnp.log(1.0 + jnp.exp(-jnp.abs(z)))
    lw = -jnp.exp(-softplus - 0.5)
    a = _sigmoid(a0 + _dot(lora_scr[:, LORA_PAD:2 * LORA_PAD], a2_ref[...]))
    g = _dot(lora_scr[:, 2 * LORA_PAD:], g2_ref[...])
    kk = k * k_k
    nrm = jnp.sqrt(_segsum(kk * kk, seg))
    kk = kk / jnp.maximum(nrm, 1e-12)
    k2 = k * (1.0 + (a - 1.0) * k_a)
    bonus = _segsum(r * k2 * r_k, seg) * v
    r_out[0] = r.astype(BF16)
    lw_out[0] = lw
    k_out[0] = k2.astype(BF16)
    v_out[0] = v.astype(BF16)
    kk_out[0] = kk.astype(BF16)
    a_out[0] = a.astype(BF16)
    g_out[0] = g.astype(BF16)
    bonus_out[0] = bonus.astype(BF16)


def _rwkv_in(x, sc, sh, mu, wl1, w_r, w_k, w_v, w2, a2, g2, vec, seg, tm=512, tn=256):
    b, s, d = x.shape
    sub = 8
    vecs = pl.BlockSpec((1, 1, d), lambda bi, i, j: (bi, 0, 0))
    col = lambda rows: pl.BlockSpec((rows, tn), lambda bi, i, j: (0, j))
    out = pl.BlockSpec((1, tm, tn), lambda bi, i, j: (bi, i, j))
    return pl.pallas_call(
        _rwkv_in_kernel,
        grid=(b, s // tm, d // tn),
        in_specs=[pl.BlockSpec((1, tm, d), lambda bi, i, j: (bi, i, 0)),
                  pl.BlockSpec((1, sub, d), lambda bi, i, j: (bi, jnp.maximum(i * (tm // sub) - 1, 0), 0)),
                  vecs, vecs,
                  pl.BlockSpec(mu.shape, lambda bi, i, j: (0, 0)),
                  pl.BlockSpec(wl1.shape, lambda bi, i, j: (0, 0)),
                  col(d), col(d), col(d), col(LORA_PAD), col(LORA_PAD), col(R_GATE_LORA),
                  col(8), pl.BlockSpec(seg.shape, lambda bi, i, j: (0, 0))],
        out_specs=[out] * 8,
        out_shape=[jax.ShapeDtypeStruct((b, s, d), F32 if i == 1 else BF16) for i in range(8)],
        scratch_shapes=[pltpu.VMEM((6, tm, d), BF16), pltpu.VMEM((tm, 2 * LORA_PAD + R_GATE_LORA), BF16)],
        compiler_params=_params("parallel", "parallel", "arbitrary"),
    )(x, x, sc, sh, mu, wl1, w_r, w_k, w_v, w2, a2, g2, vec, seg)


def _rwkv_rec_kernel(r_ref, lw_ref, k_ref, v_ref, kk_ref, a_ref, o_ref, s_scr, *, n_chunks, n_heads):
    t = pl.program_id(2)

    @pl.when(t == 0)
    def _():
        s_scr[...] = jnp.zeros_like(s_scr)

    c_len = CHUNK
    n = R_HEAD
    ri = lax.broadcasted_iota(jnp.int32, (c_len, c_len), 0)
    ci = lax.broadcasted_iota(jnp.int32, (c_len, c_len), 1)
    strict = ri > ci
    incl = ri >= ci
    tri = jnp.where(incl, 1.0, 0.0).astype(BF16)
    items = [(c, h) for c in range(n_chunks) for h in range(n_heads)]

    pre = []
    for c in range(n_chunks):
        sl = slice(c * c_len, (c + 1) * c_len)
        lw = lw_ref[0, sl, :]
        k = k_ref[0, sl, :].astype(F32)
        kk = kk_ref[0, sl, :].astype(F32)
        kb = kk * a_ref[0, sl, :].astype(F32)
        cum = _cumsum_rows(tri, lw)
        cum_c = cum[c_len - 1:c_len, :]
        p_inv = jnp.exp(-cum)
        p_hat = jnp.exp(cum_c - cum)
        pre.append(dict(
            r_t=r_ref[0, sl, :].astype(F32) * jnp.exp(cum), a_t=-kk * jnp.exp(cum - lw), b_t=kb * p_inv,
            k_t=k * p_inv, b_h=kb * p_hat, k_h=k * p_hat, p_c=jnp.exp(cum_c), v=v_ref[0, sl, :].astype(F32)))

    def head(c, h, name):
        return pre[c][name][:, h * n:(h + 1) * n]

    amat = {}
    for c, h in items:
        ar = jnp.concatenate([head(c, h, "a_t"), head(c, h, "r_t")], axis=0)
        bk = jnp.concatenate([head(c, h, "b_t"), head(c, h, "k_t")], axis=0)
        amat[c, h] = _mm(ar, bk, NT_DIMS, PASSES_PAIR)
    l_ab, m_rb, lv = {}, {}, {}
    for c, h in items:
        am = amat[c, h]
        l_ab[c, h] = jnp.where(strict, am[:c_len, :c_len], 0.0)
        m_rb[c, h] = jnp.where(incl, am[c_len:, :c_len], 0.0)
        lm = jnp.concatenate([jnp.where(strict, am[:c_len, c_len:], 0.0),
                              jnp.where(incl, am[c_len:, c_len:], 0.0)], axis=0)
        lv[c, h] = _mm(lm, head(c, h, "v"), NN_DIMS, PASSES_PAIR)
    z = {}
    for c, h in items:
        z[c, h] = jnp.concatenate([head(c, h, "a_t"), lv[c, h][:c_len], l_ab[c, h]], axis=1)
    for it in range(6):
        for c, h in items:
            zz = z[c, h]
            pw = zz[:, 2 * n:]
            if it < 5:
                x = _mm(pw, zz, NN_DIMS, PASSES_SOLVE)
                z[c, h] = jnp.concatenate([zz[:, :2 * n] + x[:, :2 * n], x[:, 2 * n:]], axis=1)
            else:
                z[c, h] = zz[:, :2 * n] + _mm(pw, zz[:, :2 * n], NN_DIMS, PASSES_SOLVE)
    state = [s_scr[h] for h in range(n_heads)]
    for c in range(n_chunks):
        u = [_mm(z[c, h][:, :n], state[h], NT_DIMS, PASSES_STATE) + z[c, h][:, n:] for h in range(n_heads)]
        o = [_mm(head(c, h, "r_t"), state[h], NT_DIMS, PASSES_STATE) + _mm(m_rb[c, h], u[h], NN_DIMS, PASSES_STATE)
             + lv[c, h][c_len:] for h in range(n_heads)]
        state = [state[h] * head(c, h, "p_c")
                 + _mm(u[h], head(c, h, "b_h"), TN_DIMS, PASSES_STATE)
                 + _mm(head(c, h, "v"), head(c, h, "k_h"), TN_DIMS, PASSES_STATE) for h in range(n_heads)]
        o_ref[0, c * c_len:(c + 1) * c_len, :] = jnp.concatenate(o, axis=1)
    for h in range(n_heads):
        s_scr[h] = state[h]


def _mm(a, b, dims=NN_DIMS, passes=1):
    if passes == 1:
        return _dot(a.astype(BF16), b.astype(BF16), dims)
    return _dot3(a, b, dims)


def _cumsum_rows(tri, x):
    hi, lo = _split(x)
    lo2 = (x - hi.astype(F32) - lo.astype(F32)).astype(BF16)
    return _dot(tri, hi) + _dot(tri, lo) + _dot(tri, lo2)


def _rwkv_rec(r, lw, k, v, kk, a, tb=256, lanes=512):
    b, s, d = r.shape
    tb = min(tb, s)
    n_heads = lanes // R_HEAD
    kern = functools.partial(_rwkv_rec_kernel, n_chunks=tb // CHUNK, n_heads=n_heads)
    blk = pl.BlockSpec((1, tb, lanes), lambda bi, hi, ti: (bi, ti, hi))
    return pl.pallas_call(
        kern,
        grid=(b, d // lanes, s // tb),
        in_specs=[blk] * 6,
        out_specs=blk,
        out_shape=jax.ShapeDtypeStruct((b, s, d), F32),
        scratch_shapes=[pltpu.VMEM((n_heads, R_HEAD, R_HEAD), F32)],
        compiler_params=_params("parallel", "parallel", "arbitrary"),
    )(r, lw, k, v, kk, a)


def _rwkv_out_kernel(o_ref, g_ref, bonus_ref, x_ref, gate_ref, gng_ref, gnb_ref, seg_ref, wo_ref,
                     out_ref, y_scr):
    d = x_ref.shape[2]
    tn = seg_ref.shape[0]
    seg = seg_ref[...]
    inv_n = 1.0 / R_HEAD
    tm = x_ref.shape[1]
    for rs in (slice(0, tm // 2), slice(tm // 2, tm)):
        for jt in range(d // tn):
            sl = slice(jt * tn, (jt + 1) * tn)
            o = o_ref[0, rs, sl]
            mu = _segsum(o, seg) * inv_n
            dv = o - mu
            var = _segsum(dv * dv, seg) * inv_n
            ln = dv * lax.rsqrt(var + R_GN_EPS)
            y = ((ln * gng_ref[:, sl] + gnb_ref[:, sl] + bonus_ref[0, rs, sl].astype(F32))
                 * g_ref[0, rs, sl].astype(F32))
            y_scr[rs, sl] = y.astype(BF16)
        out_ref[0, rs] = x_ref[0, rs] + gate_ref[0] * _dot(y_scr[rs], wo_ref[...])


def _rwkv_out(o, g, bonus, x, gate, gng, gnb, seg, w_o, tm=512):
    b, s, d = x.shape
    tok = pl.BlockSpec((1, tm, d), lambda bi, i: (bi, i, 0))
    full = lambda a: pl.BlockSpec(a.shape, lambda bi, i: (0,) * a.ndim)
    return pl.pallas_call(
        _rwkv_out_kernel,
        grid=(b, s // tm),
        in_specs=[tok, tok, tok, tok, pl.BlockSpec((1, 1, d), lambda bi, i: (bi, 0, 0)),
                  full(gng), full(gnb), full(seg), full(w_o)],
        out_specs=tok,
        out_shape=jax.ShapeDtypeStruct((b, s, d), F32),
        scratch_shapes=[pltpu.VMEM((tm, d), BF16)],
        compiler_params=_params("parallel", "parallel"),
    )(o, g, bonus, x, gate, gng, gnb, seg, w_o)


def _rwkv_mixer(x, sc, sh, gate, mu, w_r, w_k, w_v, w_o, w0, w_w1, w_w2, a0, w_a1, w_a2, w_g1, w_g2,
                k_k, k_a, r_k, gn_g, gn_b):
    d = x.shape[2]
    tn = 256
    lora = w_w1.shape[1]
    padc = lambda w: jnp.pad(w, ((0, 0), (0, LORA_PAD - lora)))
    padr = lambda w: jnp.pad(w, ((0, LORA_PAD - lora), (0, 0)))
    wl1 = jnp.concatenate([padc(w_w1), padc(w_a1), w_g1], axis=1).astype(BF16)
    vec = jnp.stack([w0, a0, k_k, k_a, r_k.reshape(-1), w0 * 0, w0 * 0, w0 * 0])
    idx = jnp.arange(tn) // R_HEAD
    seg = (idx[:, None] == idx[None, :]).astype(BF16)
    r, lw, k2, v, kk, a, g, bonus = _rwkv_in(
        x, sc, sh, mu, wl1, _to_bf16(w_r[None])[0], _to_bf16(w_k[None])[0], _to_bf16(w_v[None])[0],
        padr(w_w2).astype(BF16), padr(w_a2).astype(BF16), w_g2.astype(BF16), vec, seg, tn=tn)
    o = _rwkv_rec(r, lw, k2, v, kk, a)
    return _rwkv_out(o, g, bonus, x, gate, gn_g.reshape(1, d), gn_b.reshape(1, d), seg, _to_bf16(w_o[None])[0])


def kernel(x, c, positions, ada_w, ada_b, mlp_w1, mlp_w2, final_g, a_w_in, a_q_norm_g, a_kv_norm_g, a_w_uq,
           a_w_qidx, a_kidx_ln_g, a_kidx_ln_b, a_w_uk, a_w_uv, a_w_o, b_mu, b_w_r, b_w_k, b_w_v, b_w_o, b_w0,
           b_w_w1, b_w_w2, b_a0, b_w_a1, b_w_a2, b_w_g1, b_w_g2, b_k_k, b_k_a, b_r_k, b_gn_g, b_gn_b):
    depth = ada_w.shape[0]
    d = x.shape[2]
    mod = _ada_mod(c, ada_w, ada_b)
    w1_all, w2_all = _to_bf16(mlp_w1), _to_bf16(mlp_w2)
    for i in range(depth):
        sh1, sc1, g1, sh2, sc2, g2 = [mod[i, :, None, m * d:(m + 1) * d] for m in range(6)]
        j = i // 2
        if i % 2 == 0:
            x = _dsa_mixer(x, sc1, sh1, g1, positions, a_w_in[j], a_q_norm_g[j], a_kv_norm_g[j], a_w_uq[j],
                           a_w_qidx[j], a_kidx_ln_g[j], a_kidx_ln_b[j], a_w_uk[j], a_w_uv[j], a_w_o[j])
        else:
            x = _rwkv_mixer(x, sc1, sh1, g1, b_mu[j], b_w_r[j], b_w_k[j], b_w_v[j], b_w_o[j], b_w0[j],
                            b_w_w1[j], b_w_w2[j], b_a0[j], b_w_a1[j], b_w_a2[j], b_w_g1[j], b_w_g2[j],
                            b_k_k[j], b_k_a[j], b_r_k[j], b_gn_g[j], b_gn_b[j])
        x = _mlp(x, sc2, sh2, g2, w1_all, w2_all, i, final_g,
                 final_norm=(i == depth - 1))
    return x
```

```python
import functools

import jax
import jax.numpy as jnp
from jax import lax
from jax.experimental import pallas as pl
from jax.experimental.pallas import tpu as pltpu

F32 = jnp.float32
BF16 = jnp.bfloat16

D_MODEL = 2048
DEPTH = 2
CHUNK = 64
EPS = 1e-6
A_HEADS = 16
A_NOPE = 128
A_ROPE = 64
A_V = 128
A_Q_RANK = 512
A_KV_RANK = 256
A_KEY = A_KV_RANK + A_ROPE
IDX_HEADS = 16
IDX_DIM = 64
IDX_ROPE = 32
TOPK_MAX = 256
Q_BLOCK = 128
ROPE_BASE = 10000.0
A_SCALE = (A_NOPE + A_ROPE) ** -0.5
A_IN = A_Q_RANK + A_KV_RANK + A_ROPE + IDX_DIM + IDX_HEADS
A_IN_PAD = 1024
R_HEAD = 64
R_GN_EPS = R_HEAD * 1e-5
LORA_PAD = 128
R_GATE_LORA = 256

VMEM_LIMIT_BYTES = 56 * 1024 * 1024
LANES = 128
SUBLANES = 8
MASK_VALUE = -1e30

NT_DIMS = (((1,), (1,)), ((), ()))
TN_DIMS = (((0,), (0,)), ((), ()))
NN_DIMS = (((1,), (0,)), ((), ()))

PASSES_PAIR = 1
PASSES_SOLVE = 1
PASSES_STATE = 1


def _params(*sem):
    return pltpu.CompilerParams(dimension_semantics=sem, vmem_limit_bytes=VMEM_LIMIT_BYTES)


def _dot(a, b, dims=NN_DIMS):
    return lax.dot_general(a, b, dims, preferred_element_type=F32)


def _split(x):
    hi = x.astype(BF16)
    lo = (x - hi.astype(F32)).astype(BF16)
    return hi, lo


def _dot3(a, b, dims=NN_DIMS):
    ah, al = _split(a)
    bh, bl = _split(b)
    return _dot(ah, bh, dims) + _dot(ah, bl, dims) + _dot(al, bh, dims)


def _segsum(x, seg):
    return _dot(x.astype(BF16), seg)


def _cast_kernel(x_ref, o_ref):
    o_ref[...] = x_ref[...].astype(BF16)


def _to_bf16(w, block_bytes=8 * 1024 * 1024):
    l, r, c = w.shape
    tr = min(r, max(8, block_bytes // (4 * c)))
    assert r % tr == 0
    blk = pl.BlockSpec((1, tr, c), lambda li, i: (li, i, 0))
    return pl.pallas_call(
        _cast_kernel, grid=(l, r // tr), in_specs=[blk], out_specs=blk,
        out_shape=jax.ShapeDtypeStruct(w.shape, BF16),
        compiler_params=_params("parallel", "parallel"),
    )(w)


def _rms(x):
    return x * lax.rsqrt(jnp.mean(x * x, axis=-1, keepdims=True) + EPS)


def _rms_mod(x, sc, sh):
    return _rms(x) * (1.0 + sc) + sh


def _sigmoid(x):
    return 1.0 / (1.0 + jnp.exp(-x))


def _rope_half(x, c, s):
    d = x.shape[-1] // 2
    x1 = x[:, :d]
    x2 = x[:, d:]
    return jnp.concatenate([x1 * c - x2 * s, x2 * c + x1 * s], axis=-1)


def _mod_kernel(c_ref, w_ref, b_ref, o_ref):
    c = c_ref[...]
    a = (c * _sigmoid(c)).astype(BF16)
    o_ref[0] = _dot(a, w_ref[0].astype(BF16)) + b_ref[0]


def _ada_mod(c, ada_w, ada_b):
    depth, d, n = ada_w.shape
    b = c.shape[0]
    tn = 1024
    return pl.pallas_call(
        _mod_kernel,
        grid=(depth, n // tn),
        in_specs=[
            pl.BlockSpec((b, d), lambda l, j: (0, 0)),
            pl.BlockSpec((1, d, tn), lambda l, j: (l, 0, j)),
            pl.BlockSpec((1, 1, tn), lambda l, j: (l, 0, j)),
        ],
        out_specs=pl.BlockSpec((1, b, tn), lambda l, j: (l, 0, j)),
        out_shape=jax.ShapeDtypeStruct((depth, b, n), F32),
        compiler_params=_params("arbitrary", "arbitrary"),
    )(c, ada_w, ada_b.reshape(depth, 1, n))


def _mlp_kernel(x_ref, xn_ref, sc_ref, sh_ref, scn_ref, shn_ref, g_ref, w1_ref, w2_ref, fg_ref, o_ref,
                h_scr, acc_scr, *, final_norm):
    f = pl.program_id(2)
    last = pl.num_programs(2) - 1

    @pl.when((pl.program_id(0) == 0) & (pl.program_id(1) == 0) & (f == 0))
    def _():
        h_scr[...] = _rms_mod(x_ref[0], sc_ref[0], sh_ref[0]).astype(BF16)
        acc_scr[...] = jnp.zeros_like(acc_scr)

    def hidden():
        a = jnp.maximum(_dot(h_scr[...], w1_ref[0]), 0.0)
        return _dot((a * a).astype(BF16), w2_ref[0])

    @pl.when(f < last)
    def _():
        acc_scr[...] += hidden()

    @pl.when(f == last)
    def _():
        y = x_ref[0] + g_ref[0] * (acc_scr[...] + hidden())
        if final_norm:
            y = _rms(y) * fg_ref[...]
        o_ref[0] = y
        h_scr[...] = _rms_mod(xn_ref[0], scn_ref[0], shn_ref[0]).astype(BF16)
        acc_scr[...] = jnp.zeros_like(acc_scr)


def _mlp(x, sc, sh, g, w1, w2, layer, final_g, final_norm, tm=512, tf=1024):
    b, s, d = x.shape
    dff = w1.shape[2]
    n_i = s // tm
    kern = functools.partial(_mlp_kernel, final_norm=final_norm)

    def next_block(bi, i):
        return jnp.minimum(bi * n_i + i + 1, b * n_i - 1)

    vec = pl.BlockSpec((1, 1, d), lambda bi, i, f: (bi, 0, 0))
    vec_next = pl.BlockSpec((1, 1, d), lambda bi, i, f: (next_block(bi, i) // n_i, 0, 0))
    return pl.pallas_call(
        kern,
        grid=(b, n_i, dff // tf),
        in_specs=[
            pl.BlockSpec((1, tm, d), lambda bi, i, f: (bi, i, 0)),
            pl.BlockSpec((1, tm, d), lambda bi, i, f: (next_block(bi, i) // n_i, next_block(bi, i) % n_i, 0)),
            vec, vec, vec_next, vec_next, vec,
            pl.BlockSpec((1, d, tf), lambda bi, i, f: (layer, 0, f)),
            pl.BlockSpec((1, tf, d), lambda bi, i, f: (layer, f, 0)),
            pl.BlockSpec((1, d), lambda bi, i, f: (0, 0)),
        ],
        out_specs=pl.BlockSpec((1, tm, d), lambda bi, i, f: (bi, i, 0)),
        out_shape=jax.ShapeDtypeStruct((b, s, d), F32),
        scratch_shapes=[pltpu.VMEM((tm, d), BF16), pltpu.VMEM((tm, d), F32)],
        compiler_params=_params("arbitrary", "arbitrary", "arbitrary"),
    )(x, x, sc, sh, sc, sh, g, w1, w2, final_g.reshape(1, d))


def _dsa_in_kernel(x_ref, sc_ref, sh_ref, w_ref, qg_ref, kvg_ref, lng_ref, lnb_ref,
                   ca_ref, sa_ref, ci_ref, si_ref, cq_ref, keys_ref, kidx_ref, widx_ref):
    o_kv = A_Q_RANK
    o_kr = o_kv + A_KV_RANK
    o_ki = o_kr + A_ROPE
    o_wi = o_ki + IDX_DIM
    tm = x_ref.shape[1]
    halves = [slice(0, tm // 2), slice(tm // 2, tm)]
    hs = [_rms_mod(x_ref[0, rs], sc_ref[0], sh_ref[0]).astype(BF16) for rs in halves]
    projs = [_dot(h, w_ref[...]) for h in hs]
    for rs, proj in zip(halves, projs):
        cq_ref[0, rs] = (_rms(proj[:, :o_kv]) * qg_ref[...]).astype(BF16)
        keys_ref[0, rs, :A_KV_RANK] = (_rms(proj[:, o_kv:o_kr]) * kvg_ref[...]).astype(BF16)
        keys_ref[0, rs, A_KV_RANK:] = _rope_half(proj[:, o_kr:o_ki], ca_ref[0, rs], sa_ref[0, rs]).astype(BF16)
        ki = proj[:, o_ki:o_wi]
        mu = jnp.mean(ki, axis=-1, keepdims=True)
        kc = ki - mu
        var = jnp.mean(kc * kc, axis=-1, keepdims=True)
        ki = kc * lax.rsqrt(var + EPS) * lng_ref[...] + lnb_ref[...]
        ki = jnp.concatenate([_rope_half(ki[:, :IDX_ROPE], ci_ref[0, rs], si_ref[0, rs]), ki[:, IDX_ROPE:]],
                             axis=-1)
        kidx_ref[0, rs] = ki.astype(BF16)
        widx_ref[0, rs] = proj[:, o_wi:o_wi + IDX_HEADS] * (IDX_HEADS ** -0.5 * IDX_DIM ** -0.5)


def _dsa_in(x, sc, sh, w_in_p, qg, kvg, lng, lnb, cos_a, sin_a, cos_i, sin_i, tm=512):
    b, s, d = x.shape
    vec = pl.BlockSpec((1, 1, d), lambda bi, i: (bi, 0, 0))

    def full(a):
        return pl.BlockSpec(a.shape, lambda bi, i: (0,) * a.ndim)

    def tok(w):
        return pl.BlockSpec((1, tm, w), lambda bi, i: (bi, i, 0))

    return pl.pallas_call(
        _dsa_in_kernel,
        grid=(b, s // tm),
        in_specs=[tok(d), vec, vec, full(w_in_p), full(qg), full(kvg), full(lng), full(lnb),
                  tok(A_ROPE // 2), tok(A_ROPE // 2), tok(IDX_ROPE // 2), tok(IDX_ROPE // 2)],
        out_specs=[tok(A_Q_RANK), tok(A_KEY), tok(IDX_DIM), tok(IDX_HEADS)],
        out_shape=[jax.ShapeDtypeStruct((b, s, A_Q_RANK), BF16),
                   jax.ShapeDtypeStruct((b, s, A_KEY), BF16),
                   jax.ShapeDtypeStruct((b, s, IDX_DIM), BF16),
                   jax.ShapeDtypeStruct((b, s, IDX_HEADS), F32)],
        compiler_params=_params("parallel", "parallel"),
    )(x, sc, sh, w_in_p, qg, kvg, lng, lnb, cos_a, sin_a, cos_i, sin_i)


def _fold_kernel(a_ref, b_ref, o_ref):
    o_ref[0] = _dot3(a_ref[0], b_ref[0]).astype(BF16)


def _fold_uq_uk(w_uq_nope, w_uk_t):
    h, rq, dn = w_uq_nope.shape
    rkv = w_uk_t.shape[2]
    return pl.pallas_call(
        _fold_kernel,
        grid=(h,),
        in_specs=[pl.BlockSpec((1, rq, dn), lambda i: (i, 0, 0)),
                  pl.BlockSpec((1, dn, rkv), lambda i: (i, 0, 0))],
        out_specs=pl.BlockSpec((1, rq, rkv), lambda i: (i, 0, 0)),
        out_shape=jax.ShapeDtypeStruct((h, rq, rkv), BF16),
        compiler_params=_params("arbitrary"),
    )(w_uq_nope, w_uk_t)


def _dsa_q_kernel(cq_ref, w_ref, ca_ref, sa_ref, ci_ref, si_ref, qatt_ref, qidx_ref):
    cq = cq_ref[0]
    tm = cq.shape[0]
    ca, sa, ci, si = ca_ref[0], sa_ref[0], ci_ref[0], si_ref[0]
    one = jnp.ones((tm, IDX_DIM - IDX_ROPE), F32)
    cq_t = jnp.concatenate([ca, ca, ca, ca], axis=-1)
    sq_t = jnp.concatenate([-sa, sa, -sa, sa], axis=-1)
    ci_t = jnp.concatenate([ci, ci, one, ci, ci, one], axis=-1)
    si_t = jnp.concatenate([-si, si, 0.0 * one, -si, si, 0.0 * one], axis=-1)
    nb = tm // Q_BLOCK
    n_lat = A_HEADS * A_KV_RANK
    n_rope = A_HEADS * A_ROPE
    n_idx = IDX_HEADS * IDX_DIM
    for h in range(A_HEADS):
        lat = _dot(cq, w_ref[:, h * A_KV_RANK:(h + 1) * A_KV_RANK]) * A_SCALE
        qatt_ref[0, :, h, :, :A_KV_RANK] = lat.astype(BF16).reshape(nb, Q_BLOCK, A_KV_RANK)
    o = n_lat
    r = _dot(cq, w_ref[:, o:o + n_rope])
    rs = _dot(cq, w_ref[:, o + n_rope:o + 2 * n_rope])
    rep = n_rope // cq_t.shape[1]
    qr = ((r * jnp.tile(cq_t, (1, rep)) + rs * jnp.tile(sq_t, (1, rep))) * A_SCALE).astype(BF16)
    for h in range(A_HEADS):
        qatt_ref[0, :, h, :, A_KV_RANK:] = qr[:, h * A_ROPE:(h + 1) * A_ROPE].reshape(nb, Q_BLOCK, A_ROPE)
    o = n_lat + 2 * n_rope
    qi = _dot(cq, w_ref[:, o:o + n_idx])
    qis = _dot(cq, w_ref[:, o + n_idx:o + 2 * n_idx])
    rep = n_idx // ci_t.shape[1]
    qi = (qi * jnp.tile(ci_t, (1, rep)) + qis * jnp.tile(si_t, (1, rep))).astype(BF16)
    for h in range(IDX_HEADS):
        qidx_ref[0, :, h] = qi[:, h * IDX_DIM:(h + 1) * IDX_DIM].reshape(nb, Q_BLOCK, IDX_DIM)


def _dsa_q(cq, w_q, cos_a, sin_a, cos_i, sin_i, tm=512):
    b, s, rq = cq.shape
    nb = tm // Q_BLOCK

    def tok(w):
        return pl.BlockSpec((1, tm, w), lambda bi, i: (bi, i, 0))

    return pl.pallas_call(
        _dsa_q_kernel,
        grid=(b, s // tm),
        in_specs=[tok(rq), pl.BlockSpec(w_q.shape, lambda bi, i: (0, 0)),
                  tok(A_ROPE // 2), tok(A_ROPE // 2), tok(IDX_ROPE // 2), tok(IDX_ROPE // 2)],
        out_specs=[pl.BlockSpec((1, nb, A_HEADS, Q_BLOCK, A_KEY), lambda bi, i: (bi, i, 0, 0, 0)),
                   pl.BlockSpec((1, nb, IDX_HEADS, Q_BLOCK, IDX_DIM), lambda bi, i: (bi, i, 0, 0, 0))],
        out_shape=[jax.ShapeDtypeStruct((b, s // Q_BLOCK, A_HEADS, Q_BLOCK, A_KEY), BF16),
                   jax.ShapeDtypeStruct((b, s // Q_BLOCK, IDX_HEADS, Q_BLOCK, IDX_DIM), BF16)],
        compiler_params=_params("parallel", "parallel"),
    )(cq, w_q, cos_a, sin_a, cos_i, sin_i)


KEY_NEG_INF = -2139095041


def _key_to_float(key):
    return pltpu.bitcast(key ^ ((key >> 31) & jnp.int32(0x7FFFFFFF)), F32)


def _dsa_idx_kernel(qidx_ref, kidx_ref, widx_ref, bias_ref, score_scr, last_scr, *, topk, tk):
    qb = pl.program_id(1)
    s = kidx_ref.shape[1]
    n_tiles = ((qb + 1) * Q_BLOCK + tk - 1) // tk
    q_all = qidx_ref[0, 0].reshape(IDX_HEADS * Q_BLOCK, IDX_DIM)
    w = widx_ref[0]
    q_chunk = (qb * Q_BLOCK + lax.broadcasted_iota(jnp.int32, (tk, Q_BLOCK), 1)) // CHUNK
    key0 = lax.broadcasted_iota(jnp.int32, (tk, Q_BLOCK), 0)
    sub = min(tk, LANES)

    def tile_off(t):
        return pl.multiple_of(t * tk, tk)

    def allowed_tile(off):
        return ((key0 + off) // CHUNK) <= q_chunk

    def score_tile(t, carry):
        off = tile_off(t)
        allowed = allowed_tile(off)
        for j in range(tk // sub):
            rows = slice(j * sub, (j + 1) * sub)
            rel = _dot(kidx_ref[0, pl.ds(off + j * sub, sub), :], q_all, NT_DIMS)
            sc = jnp.zeros((sub, Q_BLOCK), F32)
            for h in range(IDX_HEADS):
                sc = sc + jnp.maximum(rel[:, h * Q_BLOCK:(h + 1) * Q_BLOCK], 0.0) * w[h:h + 1, :]
            score_scr[pl.ds(off + j * sub, sub), :] = jnp.where(allowed[rows], sc, -jnp.inf)
        return carry

    lax.fori_loop(0, n_tiles, score_tile, 0)

    cnt_rows = 8 * SUBLANES

    def count(pred):
        def body(t, acc):
            off = tile_off(t)
            hit = jnp.where(pred(score_scr[pl.ds(off, tk), :], off), 1.0, 0.0)
            for j in range(tk // cnt_rows):
                acc = acc + hit[j * cnt_rows:(j + 1) * cnt_rows]
            return acc

        acc = lax.fori_loop(0, n_tiles, body, jnp.zeros((cnt_rows, Q_BLOCK), F32))
        return jnp.sum(acc, axis=0, keepdims=True)

    kf = jnp.float32(topk)
    n_visited = (n_tiles * tk).astype(F32)

    def count_key(cand):
        cand_f = _key_to_float(cand)
        return jnp.where(cand < KEY_NEG_INF, n_visited, count(lambda sc, off: sc >= cand_f))

    thr = jnp.where(count_key(jnp.zeros((1, Q_BLOCK), jnp.int32)) >= kf, jnp.int32(0), jnp.int32(-2 ** 31))

    def bit_step(i, thr):
        cand = thr | lax.shift_left(jnp.int32(1), jnp.int32(30) - i)
        return jnp.where(count_key(cand) >= kf, cand, thr)

    thr = lax.fori_loop(0, 31, bit_step, thr)
    thr_f = _key_to_float(thr)
    take_all = thr < KEY_NEG_INF

    last_scr[...] = jnp.full((1, Q_BLOCK), s - 1, jnp.int32)
    n_ge = jnp.where(take_all, kf, count(lambda sc, off: sc >= thr_f))

    @pl.when(jnp.max(n_ge) > kf)
    def _():
        need = kf - count(lambda sc, off: sc > thr_f)

        def idx_step(i, last):
            cand = last | lax.shift_left(jnp.int32(1), jnp.int32((s - 1).bit_length() - 1) - i)
            below = count(lambda sc, off: (sc == thr_f) & (key0 + off < cand))
            return jnp.where(below < need, cand, last)

        last = lax.fori_loop(0, (s - 1).bit_length(), idx_step, jnp.zeros((1, Q_BLOCK), jnp.int32))
        last_scr[...] = jnp.where(n_ge > kf, last, s - 1)

    last = last_scr[...]

    def write_tile(t, carry):
        off = tile_off(t)
        sc = score_scr[pl.ds(off, tk), :]
        sel = ((sc > thr_f) | ((sc == thr_f) & (key0 + off <= last)) | take_all) & allowed_tile(off)
        bias_ref[0, :, pl.ds(off, tk)] = jnp.where(sel, 0.0, MASK_VALUE).T.astype(BF16)
        return carry

    lax.fori_loop(0, n_tiles, write_tile, 0)

    def mask_tile(t, carry):
        bias_ref[0, :, pl.ds(tile_off(t), tk)] = jnp.full((Q_BLOCK, tk), MASK_VALUE, BF16)
        return carry

    lax.fori_loop(n_tiles, s // tk, mask_tile, 0)


def _dsa_idx(qidx, kidx, widx_t, topk, tk=256):
    b, nb = qidx.shape[:2]
    s = kidx.shape[1]
    tk = min(tk, s)
    kern = functools.partial(_dsa_idx_kernel, topk=topk, tk=tk)
    return pl.pallas_call(
        kern,
        grid=(b, nb),
        in_specs=[pl.BlockSpec((1, 1, IDX_HEADS, Q_BLOCK, IDX_DIM), lambda bi, i: (bi, i, 0, 0, 0)),
                  pl.BlockSpec((1, s, IDX_DIM), lambda bi, i: (bi, 0, 0)),
                  pl.BlockSpec((1, IDX_HEADS, Q_BLOCK), lambda bi, i: (bi, 0, i))],
        out_specs=pl.BlockSpec((1, Q_BLOCK, s), lambda bi, i: (bi, i, 0)),
        out_shape=jax.ShapeDtypeStruct((b, s, s), BF16),
        scratch_shapes=[pltpu.VMEM((s, Q_BLOCK), F32), pltpu.VMEM((1, Q_BLOCK), jnp.int32)],
        compiler_params=_params("parallel", "parallel"),
    )(qidx, kidx, widx_t)


def _dsa_attn_kernel(q_ref, keys_ref, bias_ref, o_ref, m_scr, l_scr, acc_scr, *, tk, n_strips):
    qb = pl.program_id(1)
    hs = A_HEADS // n_strips
    rows = hs * Q_BLOCK
    n_tiles = ((qb + 1) * Q_BLOCK + tk - 1) // tk

    def tile_step(t, first):
        off = pl.multiple_of(t * tk, tk)
        kt = keys_ref[0, pl.ds(off, tk), :]
        kv = kt[:, :A_KV_RANK]
        bias = bias_ref[0, :, pl.ds(off, tk)].astype(F32)

        def scores(i):
            q = q_ref[0, 0, i * hs:(i + 1) * hs].reshape(rows, A_KEY)
            return _dot(q, kt, NT_DIMS)

        def softmax_update(i, sc):
            rs = slice(i * rows, (i + 1) * rows)
            sc = (sc.reshape(hs, Q_BLOCK, tk) + bias[None]).reshape(rows, tk)
            m_cur = jnp.max(sc, axis=1, keepdims=True)
            if first:
                m_new = jnp.broadcast_to(m_cur, (rows, LANES))
                alpha = None
            else:
                m_prev = m_scr[rs]
                m_new = jnp.maximum(m_prev, m_cur)
                alpha = jnp.exp(m_prev - m_new)
            p = jnp.exp(sc - jnp.tile(m_new, (1, tk // LANES)))
            l_cur = jnp.sum(p, axis=1, keepdims=True)
            l_scr[rs] = jnp.broadcast_to(l_cur, (rows, LANES)) if first else alpha * l_scr[rs] + l_cur
            m_scr[rs] = m_new
            return p.astype(BF16), alpha

        def values(i, p, alpha):
            rs = slice(i * rows, (i + 1) * rows)
            pv = _dot(p, kv)
            acc_scr[rs] = pv if first else acc_scr[rs] * jnp.tile(alpha, (1, A_KV_RANK // LANES)) + pv

        sc = {0: scores(0)}
        pa = {}
        for i in range(n_strips):
            if i + 1 < n_strips:
                sc[i + 1] = scores(i + 1)
            pa[i] = softmax_update(i, sc.pop(i))
            if i > 0:
                values(i - 1, *pa.pop(i - 1))
        values(n_strips - 1, *pa.pop(n_strips - 1))

    tile_step(0, True)

    def body(t, carry):
        tile_step(t, False)
        return carry

    lax.fori_loop(1, n_tiles, body, 0)
    o = acc_scr[...] / jnp.tile(l_scr[...], (1, A_KV_RANK // LANES))
    o_ref[0, 0] = o.astype(BF16).reshape(A_HEADS, Q_BLOCK, A_KV_RANK)


def _dsa_attn(qatt, keys, bias, tk=512, n_strips=8):
    b, nb = qatt.shape[:2]
    s = keys.shape[1]
    tk = min(tk, s)
    rows = A_HEADS * Q_BLOCK
    kern = functools.partial(_dsa_attn_kernel, tk=tk, n_strips=n_strips)
    return pl.pallas_call(
        kern,
        grid=(b, nb),
        in_specs=[pl.BlockSpec((1, 1, A_HEADS, Q_BLOCK, A_KEY), lambda bi, i: (bi, i, 0, 0, 0)),
                  pl.BlockSpec((1, s, A_KEY), lambda bi, i: (bi, 0, 0)),
                  pl.BlockSpec((1, Q_BLOCK, s), lambda bi, i: (bi, i, 0))],
        out_specs=pl.BlockSpec((1, 1, A_HEADS, Q_BLOCK, A_KV_RANK), lambda bi, i: (bi, i, 0, 0, 0)),
        out_shape=jax.ShapeDtypeStruct((b, nb, A_HEADS, Q_BLOCK, A_KV_RANK), BF16),
        scratch_shapes=[pltpu.VMEM((rows, LANES), F32), pltpu.VMEM((rows, LANES), F32),
                        pltpu.VMEM((rows, A_KV_RANK), F32)],
        compiler_params=_params("parallel", "parallel"),
    )(qatt, keys, bias)


def _dsa_out_kernel(ol_ref, wuv_ref, wo_ref, x_ref, g_ref, o_ref, y_scr):
    nb = ol_ref.shape[1]
    tm = nb * Q_BLOCK
    for h in range(A_HEADS):
        oh = ol_ref[0, :, h].reshape(tm, A_KV_RANK)
        y_scr[:, h * A_V:(h + 1) * A_V] = _dot(oh, wuv_ref[h]).astype(BF16)
    o_ref[0] = x_ref[0] + g_ref[0] * _dot(y_scr[...], wo_ref[...])


def _dsa_out(olat, w_uv_h, w_o, x, g, tm=512):
    b, s, d = x.shape
    nb = tm // Q_BLOCK
    return pl.pallas_call(
        _dsa_out_kernel,
        grid=(b, s // tm),
        in_specs=[pl.BlockSpec((1, nb, A_HEADS, Q_BLOCK, A_KV_RANK), lambda bi, i: (bi, i, 0, 0, 0)),
                  pl.BlockSpec(w_uv_h.shape, lambda bi, i: (0, 0, 0)),
                  pl.BlockSpec(w_o.shape, lambda bi, i: (0, 0)),
                  pl.BlockSpec((1, tm, d), lambda bi, i: (bi, i, 0)),
                  pl.BlockSpec((1, 1, d), lambda bi, i: (bi, 0, 0))],
        out_specs=pl.BlockSpec((1, tm, d), lambda bi, i: (bi, i, 0)),
        out_shape=jax.ShapeDtypeStruct((b, s, d), F32),
        scratch_shapes=[pltpu.VMEM((tm, A_HEADS * A_V), BF16)],
        compiler_params=_params("parallel", "parallel"),
    )(olat, w_uv_h, w_o, x, g)


def _rope_tables(positions):
    def angles(dim):
        inv = 1.0 / (ROPE_BASE ** (jnp.arange(0, dim, 2, dtype=F32) / dim))
        ang = positions.astype(F32)[..., None] * inv
        return jnp.cos(ang), jnp.sin(ang)

    cos_a, sin_a = angles(A_ROPE)
    cos_i, sin_i = angles(IDX_ROPE)
    return cos_a, sin_a, cos_i, sin_i


def _dsa_mixer(x, sc, sh, g, positions, w_in, qg, kvg, w_uq, w_qidx, lng, lnb, w_uk, w_uv, w_o):
    b, s, d = x.shape
    topk = min(TOPK_MAX, s // 4)
    cos_a, sin_a, cos_i, sin_i = _rope_tables(positions)
    w_in_p = jnp.pad(w_in, ((0, 0), (0, A_IN_PAD - A_IN))).astype(BF16)
    cq, keys, kidx, widx = _dsa_in(x, sc, sh, w_in_p, qg.reshape(1, -1), kvg.reshape(1, -1),
                                   lng.reshape(1, -1), lnb.reshape(1, -1), cos_a, sin_a, cos_i, sin_i)

    rq = w_uq.shape[0]
    w_uq_h = w_uq.reshape(rq, A_HEADS, A_NOPE + A_ROPE)
    w_lat = _fold_uq_uk(jnp.transpose(w_uq_h[:, :, :A_NOPE], (1, 0, 2)), jnp.transpose(w_uk, (1, 2, 0)))
    w_lat = jnp.transpose(w_lat, (1, 0, 2)).reshape(rq, A_HEADS * A_KV_RANK)
    w_r = w_uq_h[:, :, A_NOPE:]
    w_rs = jnp.concatenate([w_r[..., A_ROPE // 2:], w_r[..., :A_ROPE // 2]], axis=-1)
    w_i = w_qidx.reshape(rq, IDX_HEADS, IDX_DIM)
    hr = IDX_ROPE // 2
    w_is = jnp.concatenate([w_i[..., hr:IDX_ROPE], w_i[..., :hr], jnp.zeros_like(w_i[..., IDX_ROPE:])], axis=-1)
    w_q = jnp.concatenate([w_lat, w_r.reshape(rq, -1).astype(BF16), w_rs.reshape(rq, -1).astype(BF16),
                           w_i.reshape(rq, -1).astype(BF16), w_is.reshape(rq, -1).astype(BF16)], axis=1)
    qatt, qidx = _dsa_q(cq, w_q, cos_a, sin_a, cos_i, sin_i)

    bias = _dsa_idx(qidx, kidx, jnp.transpose(widx, (0, 2, 1)), topk)
    olat = _dsa_attn(qatt, keys, bias)
    w_uv_h = jnp.transpose(w_uv, (1, 0, 2)).astype(BF16)
    return _dsa_out(olat, w_uv_h, _to_bf16(w_o[None])[0], x, g)


def _rwkv_in_kernel(x_ref, xp_ref, sc_ref, sh_ref, mu_ref, wl1_ref, wr_ref, wk_ref, wv_ref,
                    w2_ref, a2_ref, g2_ref, vec_ref, seg_ref,
                    r_out, lw_out, k_out, v_out, kk_out, a_out, g_out, bonus_out,
                    mix_scr, lora_scr):
    i = pl.program_id(1)
    j = pl.program_id(2)

    @pl.when(j == 0)
    def _():
        sc = sc_ref[0]
        sh = sh_ref[0]
        h = _rms_mod(x_ref[0], sc, sh)
        hp = _rms_mod(xp_ref[0, SUBLANES - 1:SUBLANES, :], sc, sh)
        hp = jnp.where(i == 0, 0.0, hp)
        rows = lax.broadcasted_iota(jnp.int32, h.shape, 0)
        h_prev = jnp.where(rows == 0, hp, pltpu.roll(h, 1, axis=0))
        delta = h_prev - h
        for m in range(6):
            mix_scr[m] = (h + delta * mu_ref[m:m + 1, :]).astype(BF16)
        tw = jnp.tanh(_dot(mix_scr[1], wl1_ref[:, :LORA_PAD]))
        ta = _dot(mix_scr[4], wl1_ref[:, LORA_PAD:2 * LORA_PAD])
        tg = _sigmoid(_dot(mix_scr[5], wl1_ref[:, 2 * LORA_PAD:]))
        lora_scr[:, :LORA_PAD] = tw.astype(BF16)
        lora_scr[:, LORA_PAD:2 * LORA_PAD] = ta.astype(BF16)
        lora_scr[:, 2 * LORA_PAD:] = tg.astype(BF16)

    w0 = vec_ref[0:1, :]
    a0 = vec_ref[1:2, :]
    k_k = vec_ref[2:3, :]
    k_a = vec_ref[3:4, :]
    r_k = vec_ref[4:5, :]
    seg = seg_ref[...]
    r = _dot(mix_scr[0], wr_ref[...])
    k = _dot(mix_scr[2], wk_ref[...])
    v = _dot(mix_scr[3], wv_ref[...])
    z = -(w0 + _dot(lora_scr[:, :LORA_PAD], w2_ref[...]))
    softplus = jnp.maximum(z, 0.0) + jnp.log(1.0 + jnp.exp(-jnp.abs(z)))
    lw = -jnp.exp(-softplus - 0.5)
    a = _sigmoid(a0 + _dot(lora_scr[:, LORA_PAD:2 * LORA_PAD], a2_ref[...]))
    g = _dot(lora_scr[:, 2 * LORA_PAD:], g2_ref[...])
    kk = k * k_k
    nrm = jnp.sqrt(_segsum(kk * kk, seg))
    kk = kk / jnp.maximum(nrm, 1e-12)
    k2 = k * (1.0 + (a - 1.0) * k_a)
    bonus = _segsum(r * k2 * r_k, seg) * v
    r_out[0] = r.astype(BF16)
    lw_out[0] = lw
    k_out[0] = k2.astype(BF16)
    v_out[0] = v.astype(BF16)
    kk_out[0] = kk.astype(BF16)
    a_out[0] = a.astype(BF16)
    g_out[0] = g.astype(BF16)
    bonus_out[0] = bonus.astype(BF16)


def _rwkv_in(x, sc, sh, mu, wl1, w_r, w_k, w_v, w2, a2, g2, vec, seg, tm=512, tn=256):
    b, s, d = x.shape
    sub = SUBLANES
    vecs = pl.BlockSpec((1, 1, d), lambda bi, i, j: (bi, 0, 0))
    col = lambda rows: pl.BlockSpec((rows, tn), lambda bi, i, j: (0, j))
    out = pl.BlockSpec((1, tm, tn), lambda bi, i, j: (bi, i, j))
    return pl.pallas_call(
        _rwkv_in_kernel,
        grid=(b, s // tm, d // tn),
        in_specs=[pl.BlockSpec((1, tm, d), lambda bi, i, j: (bi, i, 0)),
                  pl.BlockSpec((1, sub, d), lambda bi, i, j: (bi, jnp.maximum(i * (tm // sub) - 1, 0), 0)),
                  vecs, vecs,
                  pl.BlockSpec(mu.shape, lambda bi, i, j: (0, 0)),
                  pl.BlockSpec(wl1.shape, lambda bi, i, j: (0, 0)),
                  col(d), col(d), col(d), col(LORA_PAD), col(LORA_PAD), col(R_GATE_LORA),
                  col(SUBLANES), pl.BlockSpec(seg.shape, lambda bi, i, j: (0, 0))],
        out_specs=[out] * 8,
        out_shape=[jax.ShapeDtypeStruct((b, s, d), F32 if i == 1 else BF16) for i in range(8)],
        scratch_shapes=[pltpu.VMEM((6, tm, d), BF16), pltpu.VMEM((tm, 2 * LORA_PAD + R_GATE_LORA), BF16)],
        compiler_params=_params("parallel", "parallel", "arbitrary"),
    )(x, x, sc, sh, mu, wl1, w_r, w_k, w_v, w2, a2, g2, vec, seg)


def _rwkv_rec_kernel(r_ref, lw_ref, k_ref, v_ref, kk_ref, a_ref, o_ref, s_scr, *, n_chunks, n_heads):
    t = pl.program_id(2)

    @pl.when(t == 0)
    def _():
        s_scr[...] = jnp.zeros_like(s_scr)

    c_len = CHUNK
    n = R_HEAD
    ri = lax.broadcasted_iota(jnp.int32, (c_len, c_len), 0)
    ci = lax.broadcasted_iota(jnp.int32, (c_len, c_len), 1)
    strict = ri > ci
    incl = ri >= ci
    tri = jnp.where(incl, 1.0, 0.0).astype(BF16)
    items = [(c, h) for c in range(n_chunks) for h in range(n_heads)]

    pre = []
    for c in range(n_chunks):
        sl = slice(c * c_len, (c + 1) * c_len)
        lw = lw_ref[0, sl, :]
        k = k_ref[0, sl, :].astype(F32)
        kk = kk_ref[0, sl, :].astype(F32)
        kb = kk * a_ref[0, sl, :].astype(F32)
        cum = _cumsum_rows(tri, lw)
        cum_c = cum[c_len - 1:c_len, :]
        p_inv = jnp.exp(-cum)
        p_hat = jnp.exp(cum_c - cum)
        pre.append(dict(
            r_t=r_ref[0, sl, :].astype(F32) * jnp.exp(cum), a_t=-kk * jnp.exp(cum - lw), b_t=kb * p_inv,
            k_t=k * p_inv, b_h=kb * p_hat, k_h=k * p_hat, p_c=jnp.exp(cum_c), v=v_ref[0, sl, :].astype(F32)))

    def head(c, h, name):
        return pre[c][name][:, h * n:(h + 1) * n]

    amat = {}
    for c, h in items:
        ar = jnp.concatenate([head(c, h, "a_t"), head(c, h, "r_t")], axis=0)
        bk = jnp.concatenate([head(c, h, "b_t"), head(c, h, "k_t")], axis=0)
        amat[c, h] = _mm(ar, bk, NT_DIMS, PASSES_PAIR)
    l_ab, m_rb, lv = {}, {}, {}
    for c, h in items:
        am = amat[c, h]
        l_ab[c, h] = jnp.where(strict, am[:c_len, :c_len], 0.0)
        m_rb[c, h] = jnp.where(incl, am[c_len:, :c_len], 0.0)
        lm = jnp.concatenate([jnp.where(strict, am[:c_len, c_len:], 0.0),
                              jnp.where(incl, am[c_len:, c_len:], 0.0)], axis=0)
        lv[c, h] = _mm(lm, head(c, h, "v"), NN_DIMS, PASSES_PAIR)
    z = {}
    for c, h in items:
        z[c, h] = jnp.concatenate([head(c, h, "a_t"), lv[c, h][:c_len], l_ab[c, h]], axis=1)
    for it in range(6):
        for c, h in items:
            zz = z[c, h]
            pw = zz[:, 2 * n:]
            if it < 5:
                x = _mm(pw, zz, NN_DIMS, PASSES_SOLVE)
                z[c, h] = jnp.concatenate([zz[:, :2 * n] + x[:, :2 * n], x[:, 2 * n:]], axis=1)
            else:
                z[c, h] = zz[:, :2 * n] + _mm(pw, zz[:, :2 * n], NN_DIMS, PASSES_SOLVE)
    state = [s_scr[h] for h in range(n_heads)]
    for c in range(n_chunks):
        u = [_mm(z[c, h][:, :n], state[h], NT_DIMS, PASSES_STATE) + z[c, h][:, n:] for h in range(n_heads)]
        o = [_mm(head(c, h, "r_t"), state[h], NT_DIMS, PASSES_STATE) + _mm(m_rb[c, h], u[h], NN_DIMS, PASSES_STATE)
             + lv[c, h][c_len:] for h in range(n_heads)]
        state = [state[h] * head(c, h, "p_c")
                 + _mm(u[h], head(c, h, "b_h"), TN_DIMS, PASSES_STATE)
                 + _mm(head(c, h, "v"), head(c, h, "k_h"), TN_DIMS, PASSES_STATE) for h in range(n_heads)]
        o_ref[0, c * c_len:(c + 1) * c_len, :] = jnp.concatenate(o, axis=1)
    for h in range(n_heads):
        s_scr[h] = state[h]


def _mm(a, b, dims=NN_DIMS, passes=1):
    if passes == 1:
        return _dot(a.astype(BF16), b.astype(BF16), dims)
    return _dot3(a, b, dims)


def _cumsum_rows(tri, x):
    hi, lo = _split(x)
    lo2 = (x - hi.astype(F32) - lo.astype(F32)).astype(BF16)
    return _dot(tri, hi) + _dot(tri, lo) + _dot(tri, lo2)


def _rwkv_rec(r, lw, k, v, kk, a, tb=256, lanes=512):
    b, s, d = r.shape
    tb = min(tb, s)
    n_heads = lanes // R_HEAD
    kern = functools.partial(_rwkv_rec_kernel, n_chunks=tb // CHUNK, n_heads=n_heads)
    blk = pl.BlockSpec((1, tb, lanes), lambda bi, hi, ti: (bi, ti, hi))
    return pl.pallas_call(
        kern,
        grid=(b, d // lanes, s // tb),
        in_specs=[blk] * 6,
        out_specs=blk,
        out_shape=jax.ShapeDtypeStruct((b, s, d), F32),
        scratch_shapes=[pltpu.VMEM((n_heads, R_HEAD, R_HEAD), F32)],
        compiler_params=_params("parallel", "parallel", "arbitrary"),
    )(r, lw, k, v, kk, a)


def _rwkv_out_kernel(o_ref, g_ref, bonus_ref, x_ref, gate_ref, gng_ref, gnb_ref, seg_ref, wo_ref,
                     out_ref, y_scr):
    d = x_ref.shape[2]
    tn = seg_ref.shape[0]
    seg = seg_ref[...]
    inv_n = 1.0 / R_HEAD
    tm = x_ref.shape[1]
    for rs in (slice(0, tm // 2), slice(tm // 2, tm)):
        for jt in range(d // tn):
            sl = slice(jt * tn, (jt + 1) * tn)
            o = o_ref[0, rs, sl]
            mu = _segsum(o, seg) * inv_n
            dv = o - mu
            var = _segsum(dv * dv, seg) * inv_n
            ln = dv * lax.rsqrt(var + R_GN_EPS)
            y = ((ln * gng_ref[:, sl] + gnb_ref[:, sl] + bonus_ref[0, rs, sl].astype(F32))
                 * g_ref[0, rs, sl].astype(F32))
            y_scr[rs, sl] = y.astype(BF16)
        out_ref[0, rs] = x_ref[0, rs] + gate_ref[0] * _dot(y_scr[rs], wo_ref[...])


def _rwkv_out(o, g, bonus, x, gate, gng, gnb, seg, w_o, tm=512):
    b, s, d = x.shape
    tok = pl.BlockSpec((1, tm, d), lambda bi, i: (bi, i, 0))
    full = lambda a: pl.BlockSpec(a.shape, lambda bi, i: (0,) * a.ndim)
    return pl.pallas_call(
        _rwkv_out_kernel,
        grid=(b, s // tm),
        in_specs=[tok, tok, tok, tok, pl.BlockSpec((1, 1, d), lambda bi, i: (bi, 0, 0)),
                  full(gng), full(gnb), full(seg), full(w_o)],
        out_specs=tok,
        out_shape=jax.ShapeDtypeStruct((b, s, d), F32),
        scratch_shapes=[pltpu.VMEM((tm, d), BF16)],
        compiler_params=_params("parallel", "parallel"),
    )(o, g, bonus, x, gate, gng, gnb, seg, w_o)


def _rwkv_mixer(x, sc, sh, gate, mu, w_r, w_k, w_v, w_o, w0, w_w1, w_w2, a0, w_a1, w_a2, w_g1, w_g2,
                k_k, k_a, r_k, gn_g, gn_b):
    d = x.shape[2]
    tn = 256
    lora = w_w1.shape[1]
    padc = lambda w: jnp.pad(w, ((0, 0), (0, LORA_PAD - lora)))
    padr = lambda w: jnp.pad(w, ((0, LORA_PAD - lora), (0, 0)))
    wl1 = jnp.concatenate([padc(w_w1), padc(w_a1), w_g1], axis=1).astype(BF16)
    zero = jnp.zeros_like(w0)
    vec = jnp.stack([w0, a0, k_k, k_a, r_k.reshape(-1), zero, zero, zero])
    idx = jnp.arange(tn) // R_HEAD
    seg = (idx[:, None] == idx[None, :]).astype(BF16)
    r, lw, k2, v, kk, a, g, bonus = _rwkv_in(
        x, sc, sh, mu, wl1, _to_bf16(w_r[None])[0], _to_bf16(w_k[None])[0], _to_bf16(w_v[None])[0],
        padr(w_w2).astype(BF16), padr(w_a2).astype(BF16), w_g2.astype(BF16), vec, seg, tn=tn)
    o = _rwkv_rec(r, lw, k2, v, kk, a)
    return _rwkv_out(o, g, bonus, x, gate, gn_g.reshape(1, d), gn_b.reshape(1, d), seg, _to_bf16(w_o[None])[0])


def kernel(x, c, positions, ada_w, ada_b, mlp_w1, mlp_w2, final_g, a_w_in, a_q_norm_g, a_kv_norm_g, a_w_uq,
           a_w_qidx, a_kidx_ln_g, a_kidx_ln_b, a_w_uk, a_w_uv, a_w_o, b_mu, b_w_r, b_w_k, b_w_v, b_w_o, b_w0,
           b_w_w1, b_w_w2, b_a0, b_w_a1, b_w_a2, b_w_g1, b_w_g2, b_k_k, b_k_a, b_r_k, b_gn_g, b_gn_b):
    depth = ada_w.shape[0]
    d = x.shape[2]
    mod = _ada_mod(c, ada_w, ada_b)
    w1_all, w2_all = _to_bf16(mlp_w1), _to_bf16(mlp_w2)
    for i in range(depth):
        sh1, sc1, g1, sh2, sc2, g2 = [mod[i, :, None, m * d:(m + 1) * d] for m in range(6)]
        j = i // 2
        if i % 2 == 0:
            x = _dsa_mixer(x, sc1, sh1, g1, positions, a_w_in[j], a_q_norm_g[j], a_kv_norm_g[j], a_w_uq[j],
                           a_w_qidx[j], a_kidx_ln_g[j], a_kidx_ln_b[j], a_w_uk[j], a_w_uv[j], a_w_o[j])
        else:
            x = _rwkv_mixer(x, sc1, sh1, g1, b_mu[j], b_w_r[j], b_w_k[j], b_w_v[j], b_w_o[j], b_w0[j],
                            b_w_w1[j], b_w_w2[j], b_a0[j], b_w_a1[j], b_w_a2[j], b_w_g1[j], b_w_g2[j],
                            b_k_k[j], b_k_a[j], b_r_k[j], b_gn_g[j], b_gn_b[j])
        x = _mlp(x, sc2, sh2, g2, w1_all, w2_all, i, final_g,
                 final_norm=(i == depth - 1))
    return x
```
